```python
import functools
import jax, jax.numpy as jnp
from jax import lax
import numpy as np

D_MODEL = 1024
BATCH = 32
SEQ = 2048
DEPTH = 1
DEC_BATCH = 16
DEC_SEQ = 16
PAST_LEN = 2048

CHUNK = 64
Q_BLOCK = 2 * CHUNK
N_HEADS = 16
HEAD_DIM = D_MODEL // N_HEADS
ATT_W = N_HEADS * HEAD_DIM
D_RNN = D_MODEL
RNN_BLOCKS = 16
RNN_BW = D_RNN // RNN_BLOCKS
CONV_W = 4
RG_C = 8.0
D_FF = ((8 * D_MODEL // 3 + 255) // 256) * 256
N_BRANCH = 2
N_MOD = 9
MACARON_W = 0.5
EPS = 1e-6
FGT_BIAS_INIT = 3.0
IN_COLS = 2 * D_RNN + 3 * ATT_W + N_HEADS + N_BRANCH * D_MODEL
IN_SPLITS = (D_RNN, 2 * D_RNN, 2 * D_RNN + ATT_W, 2 * D_RNN + 2 * ATT_W, 2 * D_RNN + 3 * ATT_W, 2 * D_RNN + 3 * ATT_W + N_HEADS, 2 * D_RNN + 3 * ATT_W + N_HEADS + D_MODEL)

kernel_name = 'hybrid_rglru_fox_macaron_stream_step'


def rms_norm(x, g):
    xf = x.astype(jnp.float32)
    inv = lax.rsqrt(jnp.mean(xf * xf, axis=-1, keepdims=True) + EPS)
    return (xf * inv).astype(x.dtype) * g


def modulate(h, shift, scale):
    return h * (1 + scale[:, None, :]) + shift[:, None, :]


def ada_mod(c, w, b, n):
    m = jax.nn.silu(c) @ w + b
    return m.reshape(c.shape[0], n, D_MODEL)


def swiglu(h, w1, w3, w2):
    return (jax.nn.silu(h @ w1) * (h @ w3)) @ w2


def ffn_sublayer(x, mod, base, g, w1, w3, w2):
    h = modulate(rms_norm(x, g), mod[:, base], mod[:, base + 1])
    gate = 1 + mod[:, base + 2][:, None, :]
    return x + MACARON_W * gate * swiglu(h, w1, w3, w2)


def causal_conv(x, buf, w, b):
    t = x.shape[1]
    xp = jnp.concatenate([buf.astype(x.dtype), x], axis=1)
    y = b + sum(xp[:, j:j + t] * w[j] for j in range(CONV_W))
    return y, xp[:, xp.shape[1] - (CONV_W - 1):]


def block_diag(x, w, b):
    bsz, t, _ = x.shape
    y = jnp.einsum('btni,nij->btnj', x.reshape(bsz, t, RNN_BLOCKS, RNN_BW), w)
    return y.reshape(bsz, t, D_RNN) + b


def rg_lru(x, h0, wa, ba, wx, bx, lam):
    f32 = jnp.float32
    r = jax.nn.sigmoid(block_diag(x, wa, ba).astype(f32))
    i = jax.nn.sigmoid(block_diag(x, wx, bx).astype(f32))
    log_a = -RG_C * r * jax.nn.softplus(-lam.astype(f32))
    a = jnp.exp(log_a)
    u = jnp.sqrt(-jnp.expm1(2.0 * log_a)) * (i * x.astype(f32))

    def step(h, au):
        a_t, u_t = au
        h = a_t * h + u_t
        return h, h

    h_last, hs = lax.scan(step, h0.astype(f32), (jnp.swapaxes(a, 0, 1), jnp.swapaxes(u, 0, 1)))
    return jnp.swapaxes(hs, 0, 1).astype(x.dtype), h_last


def fox_prompt(q, k, v, logf):
    s_len = q.shape[1]
    ft = jnp.swapaxes(jnp.cumsum(logf, axis=1), 1, 2)
    scale = HEAD_DIM ** -0.5
    outs = []
    for blk in range(s_len // Q_BLOCK):
        q0, q1 = blk * Q_BLOCK, (blk + 1) * Q_BLOCK
        s = jnp.einsum('bqhd,bkhd->bhqk', q[:, q0:q1], k[:, :q1]).astype(jnp.float32) * scale
        s = s + ft[:, :, q0:q1, None] - ft[:, :, None, :q1]
        mask = jnp.arange(q1)[None, :] <= jnp.arange(q0, q1)[:, None]
        p = jax.nn.softmax(jnp.where(mask, s, -jnp.inf), axis=-1)
        outs.append(jnp.einsum('bhqk,bkhd->bqhd', p.astype(v.dtype), v[:, :q1]))
    return jnp.concatenate(outs, axis=1)


def fox_sample(q, k, v, logf, cache_k, cache_v, cache_logf):
    past, t = cache_k.shape[1], q.shape[1]
    kk = jnp.concatenate([cache_k.astype(k.dtype), k], axis=1)
    vv = jnp.concatenate([cache_v.astype(v.dtype), v], axis=1)
    lf = jnp.concatenate([cache_logf.astype(jnp.float32), logf], axis=1)
    ft = jnp.swapaxes(jnp.cumsum(lf, axis=1), 1, 2)
    s = jnp.einsum('bqhd,bkhd->bhqk', q, kk).astype(jnp.float32) * (HEAD_DIM ** -0.5)
    s = s + ft[:, :, past:, None] - ft[:, :, None, :]
    mask = jnp.arange(past + t)[None, :] <= (past + jnp.arange(t))[:, None]
    p = jax.nn.softmax(jnp.where(mask, s, -jnp.inf), axis=-1)
    return jnp.einsum('bhqk,bkhd->bqhd', p.astype(vv.dtype), vv)


def mixer_sublayer(x, mod, p, conv_buf, h0, attend):
    bsz, t, _ = x.shape
    h = modulate(rms_norm(x, p['norm']), mod[:, 3], mod[:, 4])
    xr, gr, q, k, v, fl, ga, gb = jnp.split(h @ p['w_in'], IN_SPLITS, axis=-1)
    xc, new_buf = causal_conv(xr, conv_buf, p['conv_w'], p['conv_b'])
    y_rnn, h_last = rg_lru(xc, h0, p['rg_wa'], p['rg_ba'], p['rg_wx'], p['rg_bx'], p['rg_lam'])
    a_out = jax.nn.gelu(gr) * y_rnn
    logf = jax.nn.log_sigmoid(fl.astype(jnp.float32) + p['b_fgt'].astype(jnp.float32))
    heads = (bsz, t, N_HEADS, HEAD_DIM)
    k = k.reshape(heads)
    v = v.reshape(heads)
    att = attend(q.reshape(heads), k, v, logf).reshape(bsz, t, ATT_W)
    z = jax.nn.sigmoid(ga) * (a_out @ p['w_br_rnn']) + jax.nn.sigmoid(gb) * (att @ p['w_br_att'])
    x = x + (1 + mod[:, 5][:, None, :]) * (z @ p['w_out'])
    return x, k, v, logf, new_buf, h_last


def setup_inputs(seed: int = 0) -> dict:
    key = jax.random.key(seed)
    ks = iter(jax.random.split(key, 48))
    f32 = jnp.float32
    L = DEPTH

    def nrm(shape, s):
        return jax.random.normal(next(ks), shape, f32) * s

    x_prompt = nrm((BATCH, SEQ, D_MODEL), 1.0)
    x_sample = nrm((DEC_BATCH, DEC_SEQ, D_MODEL), 1.0)
    cache_k = nrm((L, DEC_BATCH, PAST_LEN, N_HEADS, HEAD_DIM), 1.0)
    cache_v = nrm((L, DEC_BATCH, PAST_LEN, N_HEADS, HEAD_DIM), 1.0)
    cache_logf = jax.nn.log_sigmoid(nrm((L, DEC_BATCH, PAST_LEN, N_HEADS), 1.0) + FGT_BIAS_INIT)
    state_conv = nrm((L, DEC_BATCH, CONV_W - 1, D_RNN), 1.0)
    state_rglru = nrm((L, DEC_BATCH, D_RNN), 0.5)
    c_prompt = nrm((BATCH, D_MODEL), 1.0)
    c_sample = nrm((DEC_BATCH, D_MODEL), 1.0)
    w_ada = nrm((L, D_MODEL, N_MOD * D_MODEL), 0.2 * D_MODEL ** -0.5)
    b_ada = nrm((L, N_MOD * D_MODEL), 0.01)
    ffn1_norm = 1.0 + nrm((L, D_MODEL), 0.02)
    ffn1_w1 = nrm((L, D_MODEL, D_FF), D_MODEL ** -0.5)
    ffn1_w3 = nrm((L, D_MODEL, D_FF), D_MODEL ** -0.5)
    ffn1_w2 = nrm((L, D_FF, D_MODEL), D_FF ** -0.5)
    mix_norm = 1.0 + nrm((L, D_MODEL), 0.02)
    w_in = nrm((L, D_MODEL, IN_COLS), D_MODEL ** -0.5)
    b_fgt = FGT_BIAS_INIT + nrm((L, N_HEADS), 1.0)
    conv_w = nrm((L, CONV_W, D_RNN), CONV_W ** -0.5)
    conv_b = nrm((L, D_RNN), 0.01)
    rg_wa = nrm((L, RNN_BLOCKS, RNN_BW, RNN_BW), RNN_BW ** -0.5)
    rg_ba = nrm((L, D_RNN), 0.01)
    rg_wx = nrm((L, RNN_BLOCKS, RNN_BW, RNN_BW), RNN_BW ** -0.5)
    rg_bx = nrm((L, D_RNN), 0.01)
    a0 = jax.random.uniform(next(ks), (L, D_RNN), f32, 0.9, 0.999)
    pa = a0 ** (1.0 / RG_C)
    rg_lam = jnp.log(pa) - jnp.log1p(-pa)
    w_br_rnn = nrm((L, D_RNN, D_MODEL), D_RNN ** -0.5)
    w_br_att = nrm((L, ATT_W, D_MODEL), ATT_W ** -0.5)
    w_out = nrm((L, D_MODEL, D_MODEL), D_MODEL ** -0.5)
    ffn2_norm = 1.0 + nrm((L, D_MODEL), 0.02)
    ffn2_w1 = nrm((L, D_MODEL, D_FF), D_MODEL ** -0.5)
    ffn2_w3 = nrm((L, D_MODEL, D_FF), D_MODEL ** -0.5)
    ffn2_w2 = nrm((L, D_FF, D_MODEL), D_FF ** -0.5)
    final_norm = 1.0 + nrm((D_MODEL,), 0.02)
    w_ada_f = nrm((D_MODEL, 2 * D_MODEL), 0.2 * D_MODEL ** -0.5)
    b_ada_f = nrm((2 * D_MODEL,), 0.01)
    return {'x_prompt': x_prompt, 'x_sample': x_sample, 'cache_k': cache_k, 'cache_v': cache_v,
            'cache_logf': cache_logf, 'state_conv': state_conv, 'state_rglru': state_rglru,
            'c_prompt': c_prompt, 'c_sample': c_sample, 'w_ada': w_ada, 'b_ada': b_ada,
            'ffn1_norm': ffn1_norm, 'ffn1_w1': ffn1_w1, 'ffn1_w3': ffn1_w3, 'ffn1_w2': ffn1_w2,
            'mix_norm': mix_norm, 'w_in': w_in, 'b_fgt': b_fgt, 'conv_w': conv_w, 'conv_b': conv_b,
            'rg_wa': rg_wa, 'rg_ba': rg_ba, 'rg_wx': rg_wx, 'rg_bx': rg_bx, 'rg_lam': rg_lam,
            'w_br_rnn': w_br_rnn, 'w_br_att': w_br_att, 'w_out': w_out,
            'ffn2_norm': ffn2_norm, 'ffn2_w1': ffn2_w1, 'ffn2_w3': ffn2_w3, 'ffn2_w2': ffn2_w2,
            'final_norm': final_norm, 'w_ada_f': w_ada_f, 'b_ada_f': b_ada_f}


def reference(x_prompt, x_sample, cache_k, cache_v, cache_logf, state_conv, state_rglru,
              c_prompt, c_sample, w_ada, b_ada, ffn1_norm, ffn1_w1, ffn1_w3, ffn1_w2,
              mix_norm, w_in, b_fgt, conv_w, conv_b, rg_wa, rg_ba, rg_wx, rg_bx, rg_lam,
              w_br_rnn, w_br_att, w_out, ffn2_norm, ffn2_w1, ffn2_w3, ffn2_w2,
              final_norm, w_ada_f, b_ada_f):
    xp, xs = x_prompt, x_sample
    bp = x_prompt.shape[0]
    kp_l, vp_l, lp_l, cp_l, hp_l = [], [], [], [], []
    ks_l, vs_l, ls_l, cs_l, hs_l = [], [], [], [], []
    for l in range(DEPTH):
        mp = ada_mod(c_prompt, w_ada[l], b_ada[l], N_MOD)
        ms = ada_mod(c_sample, w_ada[l], b_ada[l], N_MOD)
        xp = ffn_sublayer(xp, mp, 0, ffn1_norm[l], ffn1_w1[l], ffn1_w3[l], ffn1_w2[l])
        xs = ffn_sublayer(xs, ms, 0, ffn1_norm[l], ffn1_w1[l], ffn1_w3[l], ffn1_w2[l])
        p = {'norm': mix_norm[l], 'w_in': w_in[l], 'b_fgt': b_fgt[l], 'conv_w': conv_w[l],
             'conv_b': conv_b[l], 'rg_wa': rg_wa[l], 'rg_ba': rg_ba[l], 'rg_wx': rg_wx[l],
             'rg_bx': rg_bx[l], 'rg_lam': rg_lam[l], 'w_br_rnn': w_br_rnn[l],
             'w_br_att': w_br_att[l], 'w_out': w_out[l]}
        buf0 = jnp.zeros((bp, CONV_W - 1, D_RNN), xp.dtype)
        h00 = jnp.zeros((bp, D_RNN), jnp.float32)
        xp, kp, vp, lp, cp, hp = mixer_sublayer(xp, mp, p, buf0, h00, fox_prompt)
        attend_s = functools.partial(fox_sample, cache_k=cache_k[l], cache_v=cache_v[l],
                                     cache_logf=cache_logf[l])
        xs, ks_, vs_, ls_, cs_, hs_ = mixer_sublayer(xs, ms, p, state_conv[l], state_rglru[l], attend_s)
        xp = ffn_sublayer(xp, mp, 6, ffn2_norm[l], ffn2_w1[l], ffn2_w3[l], ffn2_w2[l])
        xs = ffn_sublayer(xs, ms, 6, ffn2_norm[l], ffn2_w1[l], ffn2_w3[l], ffn2_w2[l])
        kp_l.append(kp); vp_l.append(vp); lp_l.append(lp); cp_l.append(cp); hp_l.append(hp)
        ks_l.append(ks_); vs_l.append(vs_); ls_l.append(ls_); cs_l.append(cs_); hs_l.append(hs_)
    mfp = ada_mod(c_prompt, w_ada_f, b_ada_f, 2)
    mfs = ada_mod(c_sample, w_ada_f, b_ada_f, 2)
    y_prompt = modulate(rms_norm(xp, final_norm), mfp[:, 0], mfp[:, 1])
    y_sample = modulate(rms_norm(xs, final_norm), mfs[:, 0], mfs[:, 1])
    return (y_prompt, y_sample,
            jnp.stack(kp_l), jnp.stack(vp_l), jnp.stack(lp_l), jnp.stack(cp_l), jnp.stack(hp_l),
            jnp.stack(ks_l), jnp.stack(vs_l), jnp.stack(ls_l), jnp.stack(cs_l), jnp.stack(hs_l))
```

```python
import functools
import math

import jax
import jax.numpy as jnp
from jax import lax
from jax.experimental import pallas as pl
from jax.experimental.pallas import tpu as pltpu

F32 = jnp.float32
BF16 = jnp.bfloat16

LANE = 128
SUBLANE = 8
VMEM_BYTES_V7X = 64 * 1024 * 1024
VMEM_LIMIT = VMEM_BYTES_V7X - 8 * 1024 * 1024

EPS = 1e-6
MACARON_W = 0.5
RG_C = 8.0
N_MOD = 9

ROWS_PER_STEP = 256
SCAN_ROWS = 256
ATT_TILE = 256
CUMSUM_CHUNK = 128
NEG_BIG = -1e30


def _params(*sem):
    return pltpu.CompilerParams(dimension_semantics=sem, vmem_limit_bytes=VMEM_LIMIT)


def _resident(shape):
    nd = len(shape)
    return pl.BlockSpec(shape, lambda *_: (0,) * nd, pipeline_mode=pl.Buffered(1))


def _row_blocks(b, t, rows):
    if t >= rows:
        assert t % rows == 0, (t, rows)
        return 1, rows
    bb = max(1, min(b, rows // t))
    while b % bb:
        bb -= 1
    return bb, t


def _mod_norm(x, g, shift, scale):
    inv = lax.rsqrt(jnp.mean(x * x, axis=-1, keepdims=True) + EPS)
    return (x * inv) * g * (1.0 + scale) + shift


def _dot(a, b):
    return jnp.dot(a, b, preferred_element_type=F32)


def _log_sigmoid(x):
    return jnp.minimum(x, 0.0) - jnp.log1p(jnp.exp(-jnp.abs(x)))


def _ada_kernel(c_ref, w_ref, b_ref, o_ref):
    c = c_ref[...]
    s = (c * jax.nn.sigmoid(c)).astype(BF16)
    o_ref[...] = _dot(s, w_ref[...].astype(BF16)) + b_ref[...]


def _ada_mod(c, w, b):
    bc, d = c.shape
    n = w.shape[1]
    tn = 1024
    assert n % tn == 0
    return pl.pallas_call(
        _ada_kernel,
        grid=(n // tn,),
        in_specs=[
            pl.BlockSpec((bc, d), lambda j: (0, 0)),
            pl.BlockSpec((d, tn), lambda j: (0, j)),
            pl.BlockSpec((1, tn), lambda j: (0, j)),
        ],
        out_specs=pl.BlockSpec((bc, tn), lambda j: (0, j)),
        out_shape=jax.ShapeDtypeStruct((bc, n), F32),
        compiler_params=_params("parallel"),
        name="ada_mod",
    )(c, w, b.reshape(1, n))


def _ffn_kernel(x_ref, mod_ref, g_ref, w1_ref, w3_ref, w2_ref, *rest, base, final):
    if final:
        fg_ref, fmod_ref, o_ref = rest
    else:
        (o_ref,) = rest
    x = x_ref[...]
    bb, tm, d = x.shape
    shift = mod_ref[:, base:base + 1, :]
    scale = mod_ref[:, base + 1:base + 2, :]
    gate = 1.0 + mod_ref[:, base + 2:base + 3, :]
    hb = _mod_norm(x, g_ref[...], shift, scale).reshape(bb * tm, d).astype(BF16)
    a = _dot(hb, w1_ref[...])
    b = _dot(hb, w3_ref[...])
    t = ((a * jax.nn.sigmoid(a)) * b).astype(BF16)
    y = _dot(t, w2_ref[...]).reshape(bb, tm, d)
    out = x + MACARON_W * gate * y
    if final:
        out = _mod_norm(out, fg_ref[...], fmod_ref[:, 0:1, :], fmod_ref[:, 1:2, :])
    o_ref[...] = out


def _ffn(x, mod, g, w1, w3, w2, base, final=None):
    b, t, d = x.shape
    dff = w1.shape[1]
    bb, tm = _row_blocks(b, t, ROWS_PER_STEP)
    tok = pl.BlockSpec((bb, tm, d), lambda i, j: (i, j, 0))
    modspec = lambda n: pl.BlockSpec((bb, n, d), lambda i, j: (i, 0, 0))
    in_specs = [tok, modspec(mod.shape[1]), _resident((1, d)),
                _resident((d, dff)), _resident((d, dff)), _resident((dff, d))]
    args = [x, mod, g.reshape(1, d), w1, w3, w2]
    if final is not None:
        fg, fmod = final
        in_specs += [_resident((1, d)), modspec(fmod.shape[1])]
        args += [fg.reshape(1, d), fmod]
    return pl.pallas_call(
        functools.partial(_ffn_kernel, base=base, final=final is not None),
        grid=(b // bb, t // tm),
        in_specs=in_specs,
        out_specs=tok,
        out_shape=jax.ShapeDtypeStruct((b, t, d), F32),
        compiler_params=_params("parallel", "parallel"),
        name="ffn_final" if final is not None else "ffn",
    )(*args)


def _inproj_kernel(x_ref, mod_ref, g_ref, wm_ref, wf_ref, wg_ref, bf_ref,
                   xr_o, gr_o, q_o, k_o, v_o, lf_o, ga_o, gb_o, *, q_scale):
    x = x_ref[...]
    bb, tm, d = x.shape
    hb = _mod_norm(x, g_ref[...], mod_ref[:, 3:4, :], mod_ref[:, 4:5, :])
    hb = hb.reshape(bb * tm, d).astype(BF16)

    def proj(w_ref, idx):
        return _dot(hb, w_ref[:, idx * d:(idx + 1) * d]).reshape(bb, tm, d)

    xr_o[...] = proj(wm_ref, 0)
    gr_o[...] = proj(wm_ref, 1)
    q_o[...] = (proj(wm_ref, 2) * q_scale).astype(q_o.dtype)
    k_o[...] = proj(wm_ref, 3)
    v_o[...] = proj(wm_ref, 4)
    ga_o[...] = proj(wg_ref, 0)
    gb_o[...] = proj(wg_ref, 1)
    n_heads = lf_o.shape[-1]
    lf = _log_sigmoid(_dot(hb, wf_ref[...]) + bf_ref[...])
    lf_o[...] = lf[:, :n_heads].reshape(bb, tm, n_heads)


def _inproj(x, mod, g, w_main, w_fgt, w_gate, b_fgt_pad, n_heads, q_scale):
    b, t, d = x.shape
    bb, tm = _row_blocks(b, t, ROWS_PER_STEP)
    tok = pl.BlockSpec((bb, tm, d), lambda i, j: (i, j, 0))
    f32_out = jax.ShapeDtypeStruct((b, t, d), F32)
    return pl.pallas_call(
        functools.partial(_inproj_kernel, q_scale=q_scale),
        grid=(b // bb, t // tm),
        in_specs=[tok, pl.BlockSpec((bb, mod.shape[1], d), lambda i, j: (i, 0, 0)),
                  _resident((1, d)), _resident(w_main.shape), _resident(w_fgt.shape),
                  _resident(w_gate.shape), _resident((1, LANE))],
        out_specs=[tok, tok, tok, tok, tok,
                   pl.BlockSpec((bb, tm, n_heads), lambda i, j: (i, j, 0)), tok, tok],
        out_shape=[f32_out, f32_out, jax.ShapeDtypeStruct((b, t, d), BF16), f32_out, f32_out,
                   jax.ShapeDtypeStruct((b, t, n_heads), F32), f32_out, f32_out],
        compiler_params=_params("parallel", "parallel"),
        name="in_proj",
    )(x, mod, g.reshape(1, d), w_main, w_fgt, w_gate, b_fgt_pad)


def _linear_scan(a, u, h_in):
    t = a.shape[0]
    n = t // SUBLANE
    a3 = a.reshape(n, SUBLANE, LANE)
    u3 = u.reshape(n, SUBLANE, LANE)
    row = lax.broadcasted_iota(jnp.int32, (n, SUBLANE, LANE), 1)
    for dist in (1, 2, 4):
        keep = row >= dist
        u_prev = pltpu.roll(u3, dist, axis=1)
        a_prev = pltpu.roll(a3, dist, axis=1)
        u3 = jnp.where(keep, a3 * u_prev, 0.0) + u3
        a3 = jnp.where(keep, a3 * a_prev, a3)
    h = h_in
    outs = []
    for grp in range(n):
        hg = u3[grp] + a3[grp] * h
        outs.append(hg)
        h = hg[SUBLANE - 1:SUBLANE, :]
    return jnp.concatenate(outs, axis=0)


def _rglru_kernel(xr_ref, gr_ref, conv0_ref, h0_ref, cw_ref, cb_ref, wg_ref, ba_ref, bx_ref, lam_ref,
                  aout_ref, convo_ref, hlast_ref, buf, hcar):
    tc, c = xr_ref.shape[1], xr_ref.shape[2]
    kw = cw_ref.shape[0]
    lo = SUBLANE - (kw - 1)

    @pl.when(pl.program_id(1) == 0)
    def _():
        buf[lo:SUBLANE, :] = conv0_ref[0]
        hcar[...] = h0_ref[0]

    buf[SUBLANE:SUBLANE + tc, :] = xr_ref[0]
    xc = cb_ref[...]
    for j in range(kw):
        xc = xc + buf[lo + j:lo + j + tc, :] * cw_ref[j:j + 1, :]
    tail = buf[SUBLANE + tc - (kw - 1):SUBLANE + tc, :]
    convo_ref[0] = tail
    buf[lo:SUBLANE, :] = tail

    for ch in range(c // LANE):
        sl = slice(ch * LANE, (ch + 1) * LANE)
        xcc = xc[:, sl]
        gates = _dot(xcc.astype(BF16), wg_ref[ch])
        r = jax.nn.sigmoid(gates[:, :LANE] + ba_ref[:, sl])
        i = jax.nn.sigmoid(gates[:, LANE:] + bx_ref[:, sl])
        neg_lam = -lam_ref[:, sl]
        softplus = jnp.maximum(neg_lam, 0.0) + jnp.log1p(jnp.exp(-jnp.abs(neg_lam)))
        log_a = (-RG_C) * r * softplus
        a = jnp.exp(log_a)
        u = jnp.sqrt(-jnp.tanh(log_a) * (a * a + 1.0)) * (i * xcc)
        hs = _linear_scan(a, u, hcar[:, sl])
        hcar[:, sl] = hs[tc - 1:tc, :]
        aout_ref[0, :, sl] = (jax.nn.gelu(gr_ref[0, :, sl]) * hs).astype(aout_ref.dtype)
    hlast_ref[0] = hcar[...]


def _rglru(xr, gr, conv0, h0, conv_w, conv_b, w_gates, ba, bx, lam):
    b, t, c = xr.shape
    kw = conv_w.shape[0]
    tc = min(t, SCAN_ROWS)
    assert t % tc == 0 and tc % SUBLANE == 0 and tc >= kw - 1 and kw - 1 <= SUBLANE
    tok = pl.BlockSpec((1, tc, c), lambda i, j: (i, j, 0))
    per_b = lambda n: pl.BlockSpec((1, n, c), lambda i, j: (i, 0, 0))
    vec = _resident((1, c))
    return pl.pallas_call(
        _rglru_kernel,
        grid=(b, t // tc),
        in_specs=[tok, tok, per_b(kw - 1), per_b(1), _resident((kw, c)), vec,
                  _resident(w_gates.shape), vec, vec, vec],
        out_specs=[tok, per_b(kw - 1), per_b(1)],
        out_shape=[jax.ShapeDtypeStruct((b, t, c), BF16),
                   jax.ShapeDtypeStruct((b, kw - 1, c), F32),
                   jax.ShapeDtypeStruct((b, 1, c), F32)],
        scratch_shapes=[pltpu.VMEM((SUBLANE + tc, c), F32), pltpu.VMEM((1, c), F32)],
        compiler_params=_params("parallel", "arbitrary"),
        name="rglru",
    )(xr, gr, conv0, h0.reshape(b, 1, c), conv_w, conv_b.reshape(1, c), w_gates,
      ba.reshape(1, c), bx.reshape(1, c), lam.reshape(1, c))


def _cumsum_kernel(x_ref, o_ref):
    h, t = x_ref.shape[1], x_ref.shape[2]
    r = lax.broadcasted_iota(jnp.int32, (CUMSUM_CHUNK, CUMSUM_CHUNK), 0)
    c = lax.broadcasted_iota(jnp.int32, (CUMSUM_CHUNK, CUMSUM_CHUNK), 1)
    upper = (r <= c).astype(F32)
    carry = jnp.zeros((h, 1), F32)
    for i in range(t // CUMSUM_CHUNK):
        sl = slice(i * CUMSUM_CHUNK, (i + 1) * CUMSUM_CHUNK)
        cs = jnp.dot(x_ref[0, :, sl], upper, preferred_element_type=F32,
                     precision=lax.Precision.HIGHEST) + carry
        o_ref[0, :, sl] = cs
        carry = cs[:, CUMSUM_CHUNK - 1:CUMSUM_CHUNK]


def _cumsum_lanes(x):
    b, h, t = x.shape
    assert t % CUMSUM_CHUNK == 0
    spec = pl.BlockSpec((1, h, t), lambda i: (i, 0, 0))
    return pl.pallas_call(
        _cumsum_kernel, grid=(b,), in_specs=[spec], out_specs=spec,
        out_shape=jax.ShapeDtypeStruct((b, h, t), F32),
        compiler_params=_params("parallel"), name="fgt_cumsum",
    )(x)


def _head_column(f_blk, head):
    lane = lax.broadcasted_iota(jnp.int32, f_blk.shape, 1)
    return jnp.sum(jnp.where(lane == head, f_blk, 0.0), axis=-1, keepdims=True)


def _softmax_step(qm, kt, vt, bias, carry, mask=None):
    m_i, l_i, acc = carry
    s = lax.dot_general(qm, kt, (((1,), (1,)), ((), ())), preferred_element_type=F32) + bias
    if mask is not None:
        s = jnp.where(mask, s, -jnp.inf)
    m_new = jnp.maximum(m_i, jnp.max(s, axis=-1, keepdims=True))
    alpha = jnp.exp(m_i - m_new)
    p = jnp.exp(s - m_new)
    l_new = alpha * l_i + jnp.sum(p, axis=-1, keepdims=True)
    acc = alpha * acc + _dot(p.astype(BF16), vt)
    return m_new, l_new, acc


def _attn_prompt_kernel(q_ref, k_ref, v_ref, f_ref, ft_ref, o_ref, kb, vb, *, head_dim):
    t = q_ref.shape[1]
    tq = min(t, ATT_TILE)
    heads_per_block = LANE // head_dim
    pair = pl.program_id(1)
    kb[...] = k_ref[0].astype(BF16)
    vb[...] = v_ref[0].astype(BF16)
    lane = lax.broadcasted_iota(jnp.int32, (1, LANE), 1)
    row = lax.broadcasted_iota(jnp.int32, (tq, tq), 0)
    col = lax.broadcasted_iota(jnp.int32, (tq, tq), 1)
    causal = col <= row

    def q_tile(qi, _):
        q0 = pl.multiple_of(qi * tq, tq)
        q = q_ref[0, pl.ds(q0, tq), :]
        out = jnp.zeros((tq, LANE), F32)
        for hh in range(heads_per_block):
            head = pair * heads_per_block + hh
            in_head = (lane >= hh * head_dim) & (lane < (hh + 1) * head_dim)
            qm = jnp.where(in_head, q, jnp.zeros_like(q))
            fq = _head_column(f_ref[0, pl.ds(q0, tq), :], head)

            def kv_tile(kj, carry):
                k0 = pl.multiple_of(kj * tq, tq)
                bias = fq - ft_ref[0, pl.ds(head, 1), pl.ds(k0, tq)]
                return _softmax_step(qm, kb[pl.ds(k0, tq), :], vb[pl.ds(k0, tq), :], bias, carry)

            init = (jnp.full((tq, 1), NEG_BIG, F32), jnp.zeros((tq, 1), F32),
                    jnp.zeros((tq, LANE), F32))
            carry = lax.fori_loop(0, qi, kv_tile, init)
            bias = fq - ft_ref[0, pl.ds(head, 1), pl.ds(q0, tq)]
            _, l_i, acc = _softmax_step(qm, kb[pl.ds(q0, tq), :], vb[pl.ds(q0, tq), :], bias,
                                        carry, mask=causal)
            out = jnp.where(in_head, acc / l_i, out)
        o_ref[0, pl.ds(q0, tq), :] = out.astype(o_ref.dtype)
        return 0

    lax.fori_loop(0, t // tq, q_tile, 0)


def _attn_prompt(qs, k, v, f_col, f_row, head_dim):
    b, t, w = qs.shape
    h = f_col.shape[-1]
    assert LANE % head_dim == 0 and w % LANE == 0 and t % min(t, ATT_TILE) == 0
    tok = pl.BlockSpec((1, t, LANE), lambda i, j: (i, 0, j))
    return pl.pallas_call(
        functools.partial(_attn_prompt_kernel, head_dim=head_dim),
        grid=(b, w // LANE),
        in_specs=[tok, tok, tok,
                  pl.BlockSpec((1, t, h), lambda i, j: (i, 0, 0)),
                  pl.BlockSpec((1, h, t), lambda i, j: (i, 0, 0))],
        out_specs=tok,
        out_shape=jax.ShapeDtypeStruct((b, t, w), BF16),
        scratch_shapes=[pltpu.VMEM((t, LANE), BF16), pltpu.VMEM((t, LANE), BF16)],
        compiler_params=_params("parallel", "arbitrary"),
        name="attn_prompt",
    )(qs, k, v, f_col, f_row)


def _attn_sample_kernel(q_ref, kn_ref, vn_ref, ck_ref, cv_ref, fn_ref, fnt_ref, fct_ref, o_ref,
                        kall, vall, *, head_dim):
    ts, past = q_ref.shape[1], ck_ref.shape[1]
    heads_per_block = LANE // head_dim
    pair = pl.program_id(1)
    kall[0:past, :] = ck_ref[0].astype(BF16)
    vall[0:past, :] = cv_ref[0].astype(BF16)
    kall[past:, :] = jnp.zeros((LANE, LANE), BF16)
    vall[past:, :] = jnp.zeros((LANE, LANE), BF16)
    kall[past:past + ts, :] = kn_ref[0].astype(BF16)
    vall[past:past + ts, :] = vn_ref[0].astype(BF16)
    lane = lax.broadcasted_iota(jnp.int32, (1, LANE), 1)
    row = lax.broadcasted_iota(jnp.int32, (ts, past + LANE), 0)
    col = lax.broadcasted_iota(jnp.int32, (ts, past + LANE), 1)
    visible = col <= past + row
    q = q_ref[0]
    out = jnp.zeros((ts, LANE), F32)
    for hh in range(heads_per_block):
        head = pair * heads_per_block + hh
        in_head = (lane >= hh * head_dim) & (lane < (hh + 1) * head_dim)
        qm = jnp.where(in_head, q, jnp.zeros_like(q))
        f_cache = fct_ref[0, pl.ds(head, 1), :]
        total = f_cache[:, past - 1:past]
        f_keys = jnp.concatenate([f_cache, total + fnt_ref[0, pl.ds(head, 1), :]], axis=1)
        bias = (total + _head_column(fn_ref[0], head)) - f_keys
        init = (jnp.full((ts, 1), NEG_BIG, F32), jnp.zeros((ts, 1), F32), jnp.zeros((ts, LANE), F32))
        _, l_i, acc = _softmax_step(qm, kall[...], vall[...], bias, init, mask=visible)
        out = jnp.where(in_head, acc / l_i, out)
    o_ref[0] = out.astype(o_ref.dtype)


def _attn_sample(qs, k_new, v_new, cache_k, cache_v, fn_col, fn_row, fc_row, head_dim):
    b, ts, w = qs.shape
    past = cache_k.shape[1]
    h = fn_col.shape[-1]
    assert ts <= LANE and past % LANE == 0
    new = pl.BlockSpec((1, ts, LANE), lambda i, j: (i, 0, j))
    old = pl.BlockSpec((1, past, LANE), lambda i, j: (i, 0, j))
    return pl.pallas_call(
        functools.partial(_attn_sample_kernel, head_dim=head_dim),
        grid=(b, w // LANE),
        in_specs=[new, new, new, old, old,
                  pl.BlockSpec((1, ts, h), lambda i, j: (i, 0, 0)),
                  pl.BlockSpec((1, h, LANE), lambda i, j: (i, 0, 0)),
                  pl.BlockSpec((1, h, past), lambda i, j: (i, 0, 0))],
        out_specs=new,
        out_shape=jax.ShapeDtypeStruct((b, ts, w), BF16),
        scratch_shapes=[pltpu.VMEM((past + LANE, LANE), BF16), pltpu.VMEM((past + LANE, LANE), BF16)],
        compiler_params=_params("parallel", "arbitrary"),
        name="attn_sample",
    )(qs, k_new, v_new, cache_k, cache_v, fn_col, fn_row, fc_row)


def _merge_kernel(x_ref, mod_ref, a_ref, t_ref, ga_ref, gb_ref, wr_ref, wa_ref, wo_ref, o_ref):
    x = x_ref[...]
    bb, tm, d = x.shape
    rows = bb * tm
    pr = _dot(a_ref[...].reshape(rows, -1), wr_ref[...])
    pa = _dot(t_ref[...].reshape(rows, -1), wa_ref[...])
    z = (jax.nn.sigmoid(ga_ref[...].reshape(rows, d)) * pr
         + jax.nn.sigmoid(gb_ref[...].reshape(rows, d)) * pa)
    y = _dot(z.astype(BF16), wo_ref[...]).reshape(bb, tm, d)
    o_ref[...] = x + (1.0 + mod_ref[:, 5:6, :]) * y


def _merge(x, mod, a_out, att, ga, gb, w_br_rnn, w_br_att, w_out):
    b, t, d = x.shape
    bb, tm = _row_blocks(b, t, ROWS_PER_STEP)
    tok = lambda n: pl.BlockSpec((bb, tm, n), lambda i, j: (i, j, 0))
    return pl.pallas_call(
        _merge_kernel,
        grid=(b // bb, t // tm),
        in_specs=[tok(d), pl.BlockSpec((bb, mod.shape[1], d), lambda i, j: (i, 0, 0)),
                  tok(a_out.shape[-1]), tok(att.shape[-1]), tok(d), tok(d),
                  _resident(w_br_rnn.shape), _resident(w_br_att.shape), _resident(w_out.shape)],
        out_specs=tok(d),
        out_shape=jax.ShapeDtypeStruct((b, t, d), F32),
        compiler_params=_params("parallel", "parallel"),
        name="merge_out",
    )(x, mod, a_out, att, ga, gb, w_br_rnn, w_br_att, w_out)


def _gate_weights(wa, wx):
    nb, bw, _ = wa.shape
    per = LANE // bw

    def chunked(w):
        w = w.reshape(nb // per, per, bw, bw)
        eye = jnp.eye(per, dtype=w.dtype)
        return jnp.einsum("cpij,pq->cpiqj", w, eye).reshape(nb // per, LANE, LANE)

    return jnp.concatenate([chunked(wa), chunked(wx)], axis=-1).astype(BF16)


def _pad_lanes(x, n):
    return jnp.pad(x, [(0, 0)] * (x.ndim - 1) + [(0, n - x.shape[-1])])


def _mixer(x, mod, w, conv0, h0, attend):
    b, t, d = x.shape
    xr, gr, qs, k, v, logf, ga, gb = _inproj(
        x, mod, w["mix_norm"], w["w_main"], w["w_fgt"], w["w_gate"], w["b_fgt"],
        w["n_heads"], w["q_scale"])
    a_out, conv_state, h_last = _rglru(xr, gr, conv0, h0, w["conv_w"], w["conv_b"], w["w_gates"],
                                       w["rg_ba"], w["rg_bx"], w["rg_lam"])
    att = attend(qs, k, v, logf)
    x = _merge(x, mod, a_out, att, ga, gb, w["w_br_rnn"], w["w_br_att"], w["w_out"])
    return x, k, v, logf, conv_state, h_last.reshape(b, -1)


def kernel(x_prompt, x_sample, cache_k, cache_v, cache_logf, state_conv, state_rglru, c_prompt, c_sample, w_ada, b_ada, ffn1_norm, ffn1_w1, ffn1_w3, ffn1_w2, mix_norm, w_in, b_fgt, conv_w, conv_b, rg_wa, rg_ba, rg_wx, rg_bx, rg_lam, w_br_rnn, w_br_att, w_out, ffn2_norm, ffn2_w1, ffn2_w3, ffn2_w2, final_norm, w_ada_f, b_ada_f):
    bp, seq, d = x_prompt.shape
    bs, dec_seq, _ = x_sample.shape
    depth = w_ada.shape[0]
    n_heads = b_fgt.shape[-1]
    head_dim = cache_k.shape[-1]
    d_rnn = conv_w.shape[-1]
    kw = conv_w.shape[1]
    att_w = n_heads * head_dim
    past = cache_k.shape[2]
    assert d_rnn == d and att_w == d and n_heads <= LANE
    assert w_in.shape[-1] == 2 * d_rnn + 3 * att_w + n_heads + 2 * d

    c_all = jnp.concatenate([c_prompt, c_sample], axis=0)
    mod_f = _ada_mod(c_all, w_ada_f, b_ada_f).reshape(bp + bs, 2, d)
    xp, xs = x_prompt, x_sample
    outs_p, outs_s = [], []
    for l in range(depth):
        mod = _ada_mod(c_all, w_ada[l], b_ada[l]).reshape(bp + bs, N_MOD, d)
        mp, ms = mod[:bp], mod[bp:]
        n_main = 2 * d_rnn + 3 * att_w
        w = {
            "mix_norm": mix_norm[l], "n_heads": n_heads, "q_scale": head_dim ** -0.5,
            "w_main": w_in[l, :, :n_main].astype(BF16),
            "w_fgt": _pad_lanes(w_in[l, :, n_main:n_main + n_heads], LANE).astype(BF16),
            "w_gate": w_in[l, :, n_main + n_heads:].astype(BF16),
            "b_fgt": _pad_lanes(b_fgt[l].reshape(1, n_heads), LANE),
            "conv_w": conv_w[l], "conv_b": conv_b[l], "w_gates": _gate_weights(rg_wa[l], rg_wx[l]),
            "rg_ba": rg_ba[l], "rg_bx": rg_bx[l], "rg_lam": rg_lam[l],
            "w_br_rnn": w_br_rnn[l].astype(BF16), "w_br_att": w_br_att[l].astype(BF16),
            "w_out": w_out[l].astype(BF16),
        }
        f1 = (ffn1_norm[l], ffn1_w1[l].astype(BF16), ffn1_w3[l].astype(BF16), ffn1_w2[l].astype(BF16))
        f2 = (ffn2_norm[l], ffn2_w1[l].astype(BF16), ffn2_w3[l].astype(BF16), ffn2_w2[l].astype(BF16))
        last = l == depth - 1

        xp = _ffn(xp, mp, *f1, base=0)
        xs = _ffn(xs, ms, *f1, base=0)

        def attend_prompt(qs, k, v, logf):
            f_row = _cumsum_lanes(jnp.swapaxes(logf, 1, 2))
            return _attn_prompt(qs, k, v, jnp.swapaxes(f_row, 1, 2), f_row, head_dim)

        def attend_sample(qs, k, v, logf):
            fc_row = _cumsum_lanes(jnp.swapaxes(cache_logf[l], 1, 2))
            fn_row = _cumsum_lanes(_pad_lanes(jnp.swapaxes(logf, 1, 2), LANE))
            fn_col = jnp.swapaxes(fn_row[:, :, :dec_seq], 1, 2)
            return _attn_sample(qs, k, v, cache_k[l].reshape(bs, past, att_w),
                                cache_v[l].reshape(bs, past, att_w), fn_col, fn_row, fc_row, head_dim)

        xp, kp, vp, lp, cp, hp = _mixer(xp, mp, w, jnp.zeros((bp, kw - 1, d_rnn), F32),
                                        jnp.zeros((bp, d_rnn), F32), attend_prompt)
        xs, ks, vs, ls, cs, hs = _mixer(xs, ms, w, state_conv[l], state_rglru[l], attend_sample)

        xp = _ffn(xp, mp, *f2, base=6, final=(final_norm, mod_f[:bp]) if last else None)
        xs = _ffn(xs, ms, *f2, base=6, final=(final_norm, mod_f[bp:]) if last else None)
        heads_p = (bp, seq, n_heads, head_dim)
        heads_s = (bs, dec_seq, n_heads, head_dim)
        outs_p.append((kp.reshape(heads_p), vp.reshape(heads_p), lp, cp, hp))
        outs_s.append((ks.reshape(heads_s), vs.reshape(heads_s), ls, cs, hs))
    stack = lambda outs: tuple(jnp.stack(leaf) for leaf in zip(*outs))
    return (xp, xs) + stack(outs_p) + stack(outs_s)
```

```python
import functools
import math

import jax
import jax.numpy as jnp
from jax import lax
from jax.experimental import pallas as pl
from jax.experimental.pallas import tpu as pltpu

F32 = jnp.float32
BF16 = jnp.bfloat16

LANE = 128
SUBLANE = 8
VMEM_BYTES_V7X = 64 * 1024 * 1024
VMEM_LIMIT = VMEM_BYTES_V7X - 8 * 1024 * 1024

EPS = 1e-6
MACARON_W = 0.5
RG_C = 8.0
N_MOD = 9

ROWS_PER_STEP = 256
SCAN_ROWS = 256
ATT_TILE = 256
CUMSUM_CHUNK = 128
NEG_BIG = -1e30
LOG2_E = math.log2(math.e)


def _params(*sem):
    return pltpu.CompilerParams(dimension_semantics=sem, vmem_limit_bytes=VMEM_LIMIT)


def _resident(shape):
    nd = len(shape)
    return pl.BlockSpec(shape, lambda *_: (0,) * nd, pipeline_mode=pl.Buffered(1))


def _row_blocks(b, t, rows):
    if t >= rows:
        assert t % rows == 0, (t, rows)
        return 1, rows
    bb = max(1, min(b, rows // t))
    while b % bb:
        bb -= 1
    return bb, t


def _mod_norm(x, g, shift, scale):
    inv = lax.rsqrt(jnp.mean(x * x, axis=-1, keepdims=True) + EPS)
    return (x * inv) * g * (1.0 + scale) + shift


def _dot(a, b):
    return jnp.dot(a, b, preferred_element_type=F32)


def _log_sigmoid(x):
    return jnp.minimum(x, 0.0) - jnp.log1p(jnp.exp(-jnp.abs(x)))


def _ada_kernel(c_ref, w_ref, b_ref, o_ref):
    c = c_ref[...]
    s = (c * jax.nn.sigmoid(c)).astype(BF16)
    o_ref[...] = _dot(s, w_ref[...].astype(BF16)) + b_ref[...]


def _ada_mod(c, w, b):
    bc, d = c.shape
    n = w.shape[1]
    tn = 1024
    assert n % tn == 0
    return pl.pallas_call(
        _ada_kernel,
        grid=(n // tn,),
        in_specs=[
            pl.BlockSpec((bc, d), lambda j: (0, 0)),
            pl.BlockSpec((d, tn), lambda j: (0, j)),
            pl.BlockSpec((1, tn), lambda j: (0, j)),
        ],
        out_specs=pl.BlockSpec((bc, tn), lambda j: (0, j)),
        out_shape=jax.ShapeDtypeStruct((bc, n), F32),
        compiler_params=_params("parallel"),
        name="ada_mod",
    )(c, w, b.reshape(1, n))


def _ffn_kernel(x_ref, mod_ref, g_ref, w1_ref, w3_ref, w2_ref, *rest, base, final):
    if final:
        fg_ref, fmod_ref, o_ref = rest
    else:
        (o_ref,) = rest
    x = x_ref[...]
    bb, tm, d = x.shape
    shift = mod_ref[:, base:base + 1, :]
    scale = mod_ref[:, base + 1:base + 2, :]
    gate = 1.0 + mod_ref[:, base + 2:base + 3, :]
    hb = _mod_norm(x, g_ref[...], shift, scale).reshape(bb * tm, d).astype(BF16)
    a = _dot(hb, w1_ref[...])
    b = _dot(hb, w3_ref[...])
    t = ((a * jax.nn.sigmoid(a)) * b).astype(BF16)
    y = _dot(t, w2_ref[...]).reshape(bb, tm, d)
    out = x + MACARON_W * gate * y
    if final:
        out = _mod_norm(out, fg_ref[...], fmod_ref[:, 0:1, :], fmod_ref[:, 1:2, :])
    o_ref[...] = out


def _ffn(x, mod, g, w1, w3, w2, base, final=None):
    b, t, d = x.shape
    dff = w1.shape[1]
    bb, tm = _row_blocks(b, t, ROWS_PER_STEP)
    tok = pl.BlockSpec((bb, tm, d), lambda i, j: (i, j, 0))
    modspec = lambda n: pl.BlockSpec((bb, n, d), lambda i, j: (i, 0, 0))
    in_specs = [tok, modspec(mod.shape[1]), _resident((1, d)),
                _resident((d, dff)), _resident((d, dff)), _resident((dff, d))]
    args = [x, mod, g.reshape(1, d), w1, w3, w2]
    if final is not None:
        fg, fmod = final
        in_specs += [_resident((1, d)), modspec(fmod.shape[1])]
        args += [fg.reshape(1, d), fmod]
    return pl.pallas_call(
        functools.partial(_ffn_kernel, base=base, final=final is not None),
        grid=(b // bb, t // tm),
        in_specs=in_specs,
        out_specs=tok,
        out_shape=jax.ShapeDtypeStruct((b, t, d), F32),
        compiler_params=_params("parallel", "parallel"),
        name="ffn_final" if final is not None else "ffn",
    )(*args)


def _inproj_kernel(x_ref, mod_ref, g_ref, wm_ref, wf_ref, wg_ref, bf_ref,
                   xr_o, gr_o, q_o, k_o, v_o, lf_o, ga_o, gb_o, *, q_scale):
    x = x_ref[...]
    bb, tm, d = x.shape
    hb = _mod_norm(x, g_ref[...], mod_ref[:, 3:4, :], mod_ref[:, 4:5, :])
    hb = hb.reshape(bb * tm, d).astype(BF16)

    def proj(w_ref, idx):
        return _dot(hb, w_ref[:, idx * d:(idx + 1) * d]).reshape(bb, tm, d)

    xr_o[...] = proj(wm_ref, 0)
    gr_o[...] = proj(wm_ref, 1)
    q_o[...] = (proj(wm_ref, 2) * q_scale).astype(q_o.dtype)
    k_o[...] = proj(wm_ref, 3)
    v_o[...] = proj(wm_ref, 4)
    ga_o[...] = proj(wg_ref, 0)
    gb_o[...] = proj(wg_ref, 1)
    n_heads = lf_o.shape[-1]
    lf = _log_sigmoid(_dot(hb, wf_ref[...]) + bf_ref[...])
    lf_o[...] = lf[:, :n_heads].reshape(bb, tm, n_heads)


def _inproj(x, mod, g, w_main, w_fgt, w_gate, b_fgt_pad, n_heads, q_scale):
    b, t, d = x.shape
    bb, tm = _row_blocks(b, t, ROWS_PER_STEP)
    tok = pl.BlockSpec((bb, tm, d), lambda i, j: (i, j, 0))
    f32_out = jax.ShapeDtypeStruct((b, t, d), F32)
    return pl.pallas_call(
        functools.partial(_inproj_kernel, q_scale=q_scale),
        grid=(b // bb, t // tm),
        in_specs=[tok, pl.BlockSpec((bb, mod.shape[1], d), lambda i, j: (i, 0, 0)),
                  _resident((1, d)), _resident(w_main.shape), _resident(w_fgt.shape),
                  _resident(w_gate.shape), _resident((1, LANE))],
        out_specs=[tok, tok, tok, tok, tok,
                   pl.BlockSpec((bb, tm, n_heads), lambda i, j: (i, j, 0)), tok, tok],
        out_shape=[f32_out, f32_out, jax.ShapeDtypeStruct((b, t, d), BF16), f32_out, f32_out,
                   jax.ShapeDtypeStruct((b, t, n_heads), F32), f32_out, f32_out],
        compiler_params=_params("parallel", "parallel"),
        name="in_proj",
    )(x, mod, g.reshape(1, d), w_main, w_fgt, w_gate, b_fgt_pad)


def _linear_scan(a, u, h_in):
    t = a.shape[0]
    n = t // SUBLANE
    a3 = a.reshape(n, SUBLANE, LANE)
    u3 = u.reshape(n, SUBLANE, LANE)
    row = lax.broadcasted_iota(jnp.int32, (n, SUBLANE, LANE), 1)
    for dist in (1, 2, 4):
        keep = row >= dist
        u_prev = pltpu.roll(u3, dist, axis=1)
        a_prev = pltpu.roll(a3, dist, axis=1)
        u3 = jnp.where(keep, a3 * u_prev, 0.0) + u3
        a3 = jnp.where(keep, a3 * a_prev, a3)
    h = h_in
    outs = []
    for grp in range(n):
        hg = u3[grp] + a3[grp] * h
        outs.append(hg)
        h = hg[SUBLANE - 1:SUBLANE, :]
    return jnp.concatenate(outs, axis=0)


def _rglru_kernel(xr_ref, gr_ref, conv0_ref, h0_ref, cw_ref, cb_ref, wg_ref, ba_ref, bx_ref, lam_ref,
                  aout_ref, convo_ref, hlast_ref, buf, hcar):
    tc, c = xr_ref.shape[1], xr_ref.shape[2]
    kw = cw_ref.shape[0]
    lo = SUBLANE - (kw - 1)

    @pl.when(pl.program_id(1) == 0)
    def _():
        buf[lo:SUBLANE, :] = conv0_ref[0]
        hcar[...] = h0_ref[0]

    buf[SUBLANE:SUBLANE + tc, :] = xr_ref[0]
    xc = cb_ref[...]
    for j in range(kw):
        xc = xc + buf[lo + j:lo + j + tc, :] * cw_ref[j:j + 1, :]
    tail = buf[SUBLANE + tc - (kw - 1):SUBLANE + tc, :]
    convo_ref[0] = tail
    buf[lo:SUBLANE, :] = tail

    for ch in range(c // LANE):
        sl = slice(ch * LANE, (ch + 1) * LANE)
        xcc = xc[:, sl]
        gates = _dot(xcc.astype(BF16), wg_ref[ch])
        r = jax.nn.sigmoid(gates[:, :LANE] + ba_ref[:, sl])
        i = jax.nn.sigmoid(gates[:, LANE:] + bx_ref[:, sl])
        neg_lam = -lam_ref[:, sl]
        softplus = jnp.maximum(neg_lam, 0.0) + jnp.log1p(jnp.exp(-jnp.abs(neg_lam)))
        log_a = (-RG_C) * r * softplus
        a = jnp.exp(log_a)
        u = jnp.sqrt(-jnp.tanh(log_a) * (a * a + 1.0)) * (i * xcc)
        hs = _linear_scan(a, u, hcar[:, sl])
        hcar[:, sl] = hs[tc - 1:tc, :]
        aout_ref[0, :, sl] = (jax.nn.gelu(gr_ref[0, :, sl]) * hs).astype(aout_ref.dtype)
    hlast_ref[0] = hcar[...]


def _rglru(xr, gr, conv0, h0, conv_w, conv_b, w_gates, ba, bx, lam):
    b, t, c = xr.shape
    kw = conv_w.shape[0]
    tc = min(t, SCAN_ROWS)
    assert t % tc == 0 and tc % SUBLANE == 0 and tc >= kw - 1 and kw - 1 <= SUBLANE
    tok = pl.BlockSpec((1, tc, c), lambda i, j: (i, j, 0))
    per_b = lambda n: pl.BlockSpec((1, n, c), lambda i, j: (i, 0, 0))
    vec = _resident((1, c))
    return pl.pallas_call(
        _rglru_kernel,
        grid=(b, t // tc),
        in_specs=[tok, tok, per_b(kw - 1), per_b(1), _resident((kw, c)), vec,
                  _resident(w_gates.shape), vec, vec, vec],
        out_specs=[tok, per_b(kw - 1), per_b(1)],
        out_shape=[jax.ShapeDtypeStruct((b, t, c), BF16),
                   jax.ShapeDtypeStruct((b, kw - 1, c), F32),
                   jax.ShapeDtypeStruct((b, 1, c), F32)],
        scratch_shapes=[pltpu.VMEM((SUBLANE + tc, c), F32), pltpu.VMEM((1, c), F32)],
        compiler_params=_params("parallel", "arbitrary"),
        name="rglru",
    )(xr, gr, conv0, h0.reshape(b, 1, c), conv_w, conv_b.reshape(1, c), w_gates,
      ba.reshape(1, c), bx.reshape(1, c), lam.reshape(1, c))


def _cumsum_kernel(x_ref, o_ref):
    h, t = x_ref.shape[1], x_ref.shape[2]
    r = lax.broadcasted_iota(jnp.int32, (CUMSUM_CHUNK, CUMSUM_CHUNK), 0)
    c = lax.broadcasted_iota(jnp.int32, (CUMSUM_CHUNK, CUMSUM_CHUNK), 1)
    upper = (r <= c).astype(F32)
    carry = jnp.zeros((h, 1), F32)
    for i in range(t // CUMSUM_CHUNK):
        sl = slice(i * CUMSUM_CHUNK, (i + 1) * CUMSUM_CHUNK)
        cs = jnp.dot(x_ref[0, :, sl], upper, preferred_element_type=F32,
                     precision=lax.Precision.HIGHEST) + carry
        o_ref[0, :, sl] = cs
        carry = cs[:, CUMSUM_CHUNK - 1:CUMSUM_CHUNK]


def _cumsum_lanes(x):
    b, h, t = x.shape
    assert t % CUMSUM_CHUNK == 0
    spec = pl.BlockSpec((1, h, t), lambda i: (i, 0, 0))
    return pl.pallas_call(
        _cumsum_kernel, grid=(b,), in_specs=[spec], out_specs=spec,
        out_shape=jax.ShapeDtypeStruct((b, h, t), F32),
        compiler_params=_params("parallel"), name="fgt_cumsum",
    )(x)


def _head_column(f_blk, head):
    lane = lax.broadcasted_iota(jnp.int32, f_blk.shape, 1)
    return jnp.sum(jnp.where(lane == head, f_blk, 0.0), axis=-1, keepdims=True)


def _softmax_step(qm, kt, vt, bias, carry, mask=None):
    m_i, l_i, acc = carry
    s = lax.dot_general(qm, kt, (((1,), (1,)), ((), ())), preferred_element_type=F32) + bias
    if mask is not None:
        s = jnp.where(mask, s, -jnp.inf)
    m_new = jnp.maximum(m_i, jnp.max(s, axis=-1, keepdims=True))
    alpha = jnp.exp2(m_i - m_new)
    p = jnp.exp2(s - m_new)
    l_new = alpha * l_i + jnp.sum(p, axis=-1, keepdims=True)
    acc = alpha * acc + _dot(p.astype(BF16), vt)
    return m_new, l_new, acc


def _split3(x):
    hi = x.astype(BF16).astype(F32)
    rest = x - hi
    mid = rest.astype(BF16).astype(F32)
    lo = (rest - mid).astype(BF16).astype(F32)
    return hi, mid, lo


def _bias_lanes(f_rows):
    heads, t = f_rows.shape
    n = 3 * heads
    terms = _split3(f_rows)
    rows = 2 * SUBLANE
    assert 2 * n <= rows
    row = lax.broadcasted_iota(jnp.int32, (rows, t), 0)
    packed = jnp.zeros((rows, t), F32)
    for hh in range(heads):
        for j in range(3):
            term = terms[j][hh:hh + 1, :]
            packed = jnp.where((row == 3 * hh + j) | (row == n + 3 * hh + j), term, packed)
    cols = jnp.concatenate([packed, jnp.zeros((LANE - rows, t), F32)], axis=0).T
    lane = lax.broadcasted_iota(jnp.int32, (1, LANE), 1)
    key_lanes = jnp.where(lane < n, -cols, jnp.where(lane < 2 * n, 1.0, 0.0))
    query_lanes = jnp.where(lane < n, 1.0, jnp.where(lane < 2 * n, cols, 0.0))
    return key_lanes, query_lanes


def _qk_scores(kt, queries):
    return [lax.dot_general(kt, qa, (((1,), (1,)), ((), ())), preferred_element_type=F32)
            for qa in queries]


def _softmax_pv(scores, values, states, masks):
    partial = []
    for s, (m_i, l_i, acc), mask in zip(scores, states, masks):
        if mask is not None:
            s = jnp.where(mask, s, -jnp.inf)
        m_new = jnp.maximum(m_i, jnp.max(s, axis=0, keepdims=True))
        alpha = jnp.exp2(m_i - m_new)
        p = jnp.exp2(s - m_new)
        l_new = alpha * l_i + jnp.sum(p, axis=0, keepdims=True)
        partial.append((m_new, l_new, alpha, p.astype(BF16)))
    return [(m_new, l_new, alpha * acc + _dot(vt, p))
            for (m_new, l_new, alpha, p), (_, _, acc), vt in zip(partial, states, values)]


def _merge_softmax(a, b):
    (m_a, l_a, acc_a), (m_b, l_b, acc_b) = a, b
    m = jnp.maximum(m_a, m_b)
    w_a = jnp.exp2(m_a - m)
    w_b = jnp.exp2(m_b - m)
    return m, w_a * l_a + w_b * l_b, w_a * acc_a + w_b * acc_b


def _attn_prompt_kernel(q_ref, k_ref, v_ref, f_ref, o_ref, kaug, qext, vt_s, s_buf, *, head_dim):
    t = q_ref.shape[1]
    tq = ATT_TILE
    heads = LANE // head_dim
    lane = lax.broadcasted_iota(jnp.int32, (1, LANE), 1)
    sub = lax.broadcasted_iota(jnp.int32, (LANE, 1), 0)
    key_pos = lax.broadcasted_iota(jnp.int32, (tq, tq), 0)
    qry_pos = lax.broadcasted_iota(jnp.int32, (tq, tq), 1)
    causal = key_pos <= qry_pos

    key_lanes, query_lanes = _bias_lanes(f_ref[0, 0] * LOG2_E)
    kaug[:, :LANE] = k_ref[0].astype(BF16)
    kaug[:, LANE:] = key_lanes.astype(BF16)
    qext[...] = query_lanes.astype(BF16)
    vt_s[...] = v_ref[0].T.astype(BF16)

    def query_tile(q0):
        q = q_ref[0, pl.ds(q0, tq), :]
        extra = qext[pl.ds(q0, tq), :]
        out = []
        for hh in range(heads):
            in_head = (lane >= hh * head_dim) & (lane < (hh + 1) * head_dim)
            lo = 3 * hh
            hi = 3 * heads + 3 * hh
            own = ((lane >= lo) & (lane < lo + 3)) | ((lane >= hi) & (lane < hi + 3))
            out.append(jnp.concatenate([jnp.where(in_head, q, jnp.zeros_like(q)),
                                        jnp.where(own, extra, jnp.zeros_like(extra))], axis=1))
        return out

    def init_state():
        return (jnp.full((1, tq), NEG_BIG, F32), jnp.zeros((1, tq), F32), jnp.zeros((LANE, tq), F32))

    def finish(q0, states):
        out_t = jnp.zeros((LANE, tq), F32)
        for hh, (_, l_i, acc) in enumerate(states):
            in_head = (sub >= hh * head_dim) & (sub < (hh + 1) * head_dim)
            out_t = jnp.where(in_head, acc / l_i, out_t)
        o_ref[0, pl.ds(q0, tq), :] = out_t.T.astype(o_ref.dtype)

    def keys(k0):
        return kaug[pl.ds(k0, tq), :]

    def values(k0):
        return vt_s[:, pl.ds(k0, tq)]

    def tile_pair(a, _):
        q_lo = pl.multiple_of(a * 2 * tq, 2 * tq)
        q_hi = pl.multiple_of(q_lo + tq, tq)
        q_lo_heads, q_hi_heads = query_tile(q_lo), query_tile(q_hi)
        queries = q_lo_heads + q_hi_heads

        n_chain = len(queries)
        unmasked = [None] * n_chain

        def put_scores(slot, k0):
            for c, s in enumerate(_qk_scores(keys(k0), queries)):
                s_buf[slot, c] = s

        def get_scores(slot):
            return [s_buf[slot, c] for c in range(n_chain)]

        def kv_tiles(i, states):
            k0 = pl.multiple_of(2 * i * tq, 2 * tq)
            k1 = pl.multiple_of(k0 + tq, tq)
            put_scores(1, k1)
            states = _softmax_pv(get_scores(0), [values(k0)] * n_chain, states, unmasked)
            put_scores(0, pl.multiple_of(k1 + tq, tq))
            return _softmax_pv(get_scores(1), [values(k1)] * n_chain, states, unmasked)

        put_scores(0, 0)
        states = lax.fori_loop(0, a, kv_tiles, [init_state() for _ in queries])
        last = _qk_scores(keys(q_hi), q_hi_heads)
        states = _softmax_pv(
            get_scores(0) + last,
            [values(q_lo)] * n_chain + [values(q_hi)] * heads,
            states + [init_state() for _ in range(heads)],
            [causal] * heads + [None] * heads + [causal] * heads)
        finish(q_lo, states[:heads])
        finish(q_hi, [_merge_softmax(states[heads + hh], states[n_chain + hh]) for hh in range(heads)])
        return 0

    lax.fori_loop(0, t // (2 * tq), tile_pair, 0)


def _attn_prompt(qs, k, v, f_row, head_dim):
    b, t, w = qs.shape
    h = f_row.shape[1]
    heads = LANE // head_dim
    assert LANE % head_dim == 0 and w % LANE == 0 and t % (2 * ATT_TILE) == 0 and h % heads == 0
    tok = pl.BlockSpec((1, t, LANE), lambda i, j: (i, 0, j))
    return pl.pallas_call(
        functools.partial(_attn_prompt_kernel, head_dim=head_dim),
        grid=(b, w // LANE),
        in_specs=[tok, tok, tok, pl.BlockSpec((1, 1, heads, t), lambda i, j: (i, j, 0, 0))],
        out_specs=tok,
        out_shape=jax.ShapeDtypeStruct((b, t, w), BF16),
        scratch_shapes=[pltpu.VMEM((t, 2 * LANE), BF16), pltpu.VMEM((t, LANE), BF16),
                        pltpu.VMEM((LANE, t), BF16),
                        pltpu.VMEM((2, 2 * heads, ATT_TILE, ATT_TILE), F32)],
        compiler_params=_params("parallel", "arbitrary"),
        name="attn_prompt",
    )(qs, k, v, f_row.reshape(b, h // heads, heads, t))


def _attn_sample_kernel(q_ref, kn_ref, vn_ref, ck_ref, cv_ref, fn_ref, fnt_ref, fct_ref, o_ref,
                        kall, vall, *, head_dim):
    ts, past = q_ref.shape[1], ck_ref.shape[1]
    heads_per_block = LANE // head_dim
    pair = pl.program_id(1)
    kall[0:past, :] = ck_ref[0].astype(BF16)
    vall[0:past, :] = cv_ref[0].astype(BF16)
    kall[past:, :] = jnp.zeros((LANE, LANE), BF16)
    vall[past:, :] = jnp.zeros((LANE, LANE), BF16)
    kall[past:past + ts, :] = kn_ref[0].astype(BF16)
    vall[past:past + ts, :] = vn_ref[0].astype(BF16)
    lane = lax.broadcasted_iota(jnp.int32, (1, LANE), 1)
    row = lax.broadcasted_iota(jnp.int32, (ts, past + LANE), 0)
    col = lax.broadcasted_iota(jnp.int32, (ts, past + LANE), 1)
    visible = col <= past + row
    q = q_ref[0]
    out = jnp.zeros((ts, LANE), F32)
    for hh in range(heads_per_block):
        head = pair * heads_per_block + hh
        in_head = (lane >= hh * head_dim) & (lane < (hh + 1) * head_dim)
        qm = jnp.where(in_head, q, jnp.zeros_like(q))
        f_cache = fct_ref[0, pl.ds(head, 1), :]
        total = f_cache[:, past - 1:past]
        f_keys = jnp.concatenate([f_cache, total + fnt_ref[0, pl.ds(head, 1), :]], axis=1)
        bias = ((total + _head_column(fn_ref[0], head)) - f_keys) * LOG2_E
        init = (jnp.full((ts, 1), NEG_BIG, F32), jnp.zeros((ts, 1), F32), jnp.zeros((ts, LANE), F32))
        _, l_i, acc = _softmax_step(qm, kall[...], vall[...], bias, init, mask=visible)
        out = jnp.where(in_head, acc / l_i, out)
    o_ref[0] = out.astype(o_ref.dtype)


def _attn_sample(qs, k_new, v_new, cache_k, cache_v, fn_col, fn_row, fc_row, head_dim):
    b, ts, w = qs.shape
    past = cache_k.shape[1]
    h = fn_col.shape[-1]
    assert ts <= LANE and past % LANE == 0
    new = pl.BlockSpec((1, ts, LANE), lambda i, j: (i, 0, j))
    old = pl.BlockSpec((1, past, LANE), lambda i, j: (i, 0, j))
    return pl.pallas_call(
        functools.partial(_attn_sample_kernel, head_dim=head_dim),
        grid=(b, w // LANE),
        in_specs=[new, new, new, old, old,
                  pl.BlockSpec((1, ts, h), lambda i, j: (i, 0, 0)),
                  pl.BlockSpec((1, h, LANE), lambda i, j: (i, 0, 0)),
                  pl.BlockSpec((1, h, past), lambda i, j: (i, 0, 0))],
        out_specs=new,
        out_shape=jax.ShapeDtypeStruct((b, ts, w), BF16),
        scratch_shapes=[pltpu.VMEM((past + LANE, LANE), BF16), pltpu.VMEM((past + LANE, LANE), BF16)],
        compiler_params=_params("parallel", "arbitrary"),
        name="attn_sample",
    )(qs, k_new, v_new, cache_k, cache_v, fn_col, fn_row, fc_row)


def _merge_kernel(x_ref, mod_ref, a_ref, t_ref, ga_ref, gb_ref, wr_ref, wa_ref, wo_ref, o_ref):
    x = x_ref[...]
    bb, tm, d = x.shape
    rows = bb * tm
    pr = _dot(a_ref[...].reshape(rows, -1), wr_ref[...])
    pa = _dot(t_ref[...].reshape(rows, -1), wa_ref[...])
    z = (jax.nn.sigmoid(ga_ref[...].reshape(rows, d)) * pr
         + jax.nn.sigmoid(gb_ref[...].reshape(rows, d)) * pa)
    y = _dot(z.astype(BF16), wo_ref[...]).reshape(bb, tm, d)
    o_ref[...] = x + (1.0 + mod_ref[:, 5:6, :]) * y


def _merge(x, mod, a_out, att, ga, gb, w_br_rnn, w_br_att, w_out):
    b, t, d = x.shape
    bb, tm = _row_blocks(b, t, ROWS_PER_STEP)
    tok = lambda n: pl.BlockSpec((bb, tm, n), lambda i, j: (i, j, 0))
    return pl.pallas_call(
        _merge_kernel,
        grid=(b // bb, t // tm),
        in_specs=[tok(d), pl.BlockSpec((bb, mod.shape[1], d), lambda i, j: (i, 0, 0)),
                  tok(a_out.shape[-1]), tok(att.shape[-1]), tok(d), tok(d),
                  _resident(w_br_rnn.shape), _resident(w_br_att.shape), _resident(w_out.shape)],
        out_specs=tok(d),
        out_shape=jax.ShapeDtypeStruct((b, t, d), F32),
        compiler_params=_params("parallel", "parallel"),
        name="merge_out",
    )(x, mod, a_out, att, ga, gb, w_br_rnn, w_br_att, w_out)


def _gate_weights(wa, wx):
    nb, bw, _ = wa.shape
    per = LANE // bw

    def chunked(w):
        w = w.reshape(nb // per, per, bw, bw)
        eye = jnp.eye(per, dtype=w.dtype)
        return jnp.einsum("cpij,pq->cpiqj", w, eye).reshape(nb // per, LANE, LANE)

    return jnp.concatenate([chunked(wa), chunked(wx)], axis=-1).astype(BF16)


def _pad_lanes(x, n):
    return jnp.pad(x, [(0, 0)] * (x.ndim - 1) + [(0, n - x.shape[-1])])


def _mixer(x, mod, w, conv0, h0, attend):
    b, t, d = x.shape
    xr, gr, qs, k, v, logf, ga, gb = _inproj(
        x, mod, w["mix_norm"], w["w_main"], w["w_fgt"], w["w_gate"], w["b_fgt"],
        w["n_heads"], w["q_scale"])
    a_out, conv_state, h_last = _rglru(xr, gr, conv0, h0, w["conv_w"], w["conv_b"], w["w_gates"],
                                       w["rg_ba"], w["rg_bx"], w["rg_lam"])
    att = attend(qs, k, v, logf)
    x = _merge(x, mod, a_out, att, ga, gb, w["w_br_rnn"], w["w_br_att"], w["w_out"])
    return x, k, v, logf, conv_state, h_last.reshape(b, -1)


def kernel(x_prompt, x_sample, cache_k, cache_v, cache_logf, state_conv, state_rglru, c_prompt, c_sample, w_ada, b_ada, ffn1_norm, ffn1_w1, ffn1_w3, ffn1_w2, mix_norm, w_in, b_fgt, conv_w, conv_b, rg_wa, rg_ba, rg_wx, rg_bx, rg_lam, w_br_rnn, w_br_att, w_out, ffn2_norm, ffn2_w1, ffn2_w3, ffn2_w2, final_norm, w_ada_f, b_ada_f):
    bp, seq, d = x_prompt.shape
    bs, dec_seq, _ = x_sample.shape
    depth = w_ada.shape[0]
    n_heads = b_fgt.shape[-1]
    head_dim = cache_k.shape[-1]
    d_rnn = conv_w.shape[-1]
    kw = conv_w.shape[1]
    att_w = n_heads * head_dim
    past = cache_k.shape[2]
    assert d_rnn == d and att_w == d and n_heads <= LANE
    assert w_in.shape[-1] == 2 * d_rnn + 3 * att_w + n_heads + 2 * d

    c_all = jnp.concatenate([c_prompt, c_sample], axis=0)
    mod_f = _ada_mod(c_all, w_ada_f, b_ada_f).reshape(bp + bs, 2, d)
    xp, xs = x_prompt, x_sample
    outs_p, outs_s = [], []
    for l in range(depth):
        mod = _ada_mod(c_all, w_ada[l], b_ada[l]).reshape(bp + bs, N_MOD, d)
        mp, ms = mod[:bp], mod[bp:]
        n_main = 2 * d_rnn + 3 * att_w
        w = {
            "mix_norm": mix_norm[l], "n_heads": n_heads, "q_scale": head_dim ** -0.5 * LOG2_E,
            "w_main": w_in[l, :, :n_main].astype(BF16),
            "w_fgt": _pad_lanes(w_in[l, :, n_main:n_main + n_heads], LANE).astype(BF16),
            "w_gate": w_in[l, :, n_main + n_heads:].astype(BF16),
            "b_fgt": _pad_lanes(b_fgt[l].reshape(1, n_heads), LANE),
            "conv_w": conv_w[l], "conv_b": conv_b[l], "w_gates": _gate_weights(rg_wa[l], rg_wx[l]),
            "rg_ba": rg_ba[l], "rg_bx": rg_bx[l], "rg_lam": rg_lam[l],
            "w_br_rnn": w_br_rnn[l].astype(BF16), "w_br_att": w_br_att[l].astype(BF16),
            "w_out": w_out[l].astype(BF16),
        }
        f1 = (ffn1_norm[l], ffn1_w1[l].astype(BF16), ffn1_w3[l].astype(BF16), ffn1_w2[l].astype(BF16))
        f2 = (ffn2_norm[l], ffn2_w1[l].astype(BF16), ffn2_w3[l].astype(BF16), ffn2_w2[l].astype(BF16))
        last = l == depth - 1

        xp = _ffn(xp, mp, *f1, base=0)
        xs = _ffn(xs, ms, *f1, base=0)

        def attend_prompt(qs, k, v, logf):
            return _attn_prompt(qs, k, v, _cumsum_lanes(jnp.swapaxes(logf, 1, 2)), head_dim)

        def attend_sample(qs, k, v, logf):
            fc_row = _cumsum_lanes(jnp.swapaxes(cache_logf[l], 1, 2))
            fn_row = _cumsum_lanes(_pad_lanes(jnp.swapaxes(logf, 1, 2), LANE))
            fn_col = jnp.swapaxes(fn_row[:, :, :dec_seq], 1, 2)
            return _attn_sample(qs, k, v, cache_k[l].reshape(bs, past, att_w),
                                cache_v[l].reshape(bs, past, att_w), fn_col, fn_row, fc_row, head_dim)

        xp, kp, vp, lp, cp, hp = _mixer(xp, mp, w, jnp.zeros((bp, kw - 1, d_rnn), F32),
                                        jnp.zeros((bp, d_rnn), F32), attend_prompt)
        xs, ks, vs, ls, cs, hs = _mixer(xs, ms, w, state_conv[l], state_rglru[l], attend_sample)

        xp = _ffn(xp, mp, *f2, base=6, final=(final_norm, mod_f[:bp]) if last else None)
        xs = _ffn(xs, ms, *f2, base=6, final=(final_norm, mod_f[bp:]) if last else None)
        heads_p = (bp, seq, n_heads, head_dim)
        heads_s = (bs, dec_seq, n_heads, head_dim)
        outs_p.append((kp.reshape(heads_p), vp.reshape(heads_p), lp, cp, hp))
        outs_s.append((ks.reshape(heads_s), vs.reshape(heads_s), ls, cs, hs))
    stack = lambda outs: tuple(jnp.stack(leaf) for leaf in zip(*outs))
    return (xp, xs) + stack(outs_p) + stack(outs_s)
```

```python
import functools
import math

import jax
import jax.numpy as jnp
from jax import lax
from jax.experimental import pallas as pl
from jax.experimental.pallas import tpu as pltpu

F32 = jnp.float32
BF16 = jnp.bfloat16

LANE = 128
SUBLANE = 8
VMEM_BYTES_V7X = 64 * 1024 * 1024
VMEM_LIMIT = VMEM_BYTES_V7X - 8 * 1024 * 1024

EPS = 1e-6
MACARON_W = 0.5
RG_C = 8.0
N_MOD = 9

ROWS_PER_STEP = 256
SCAN_ROWS = 256
ATT_TILE = 256
CUMSUM_CHUNK = 128
NEG_BIG = -1e30
LOG2_E = math.log2(math.e)


def _params(*sem):
    return pltpu.CompilerParams(dimension_semantics=sem, vmem_limit_bytes=VMEM_LIMIT)


def _resident(shape):
    nd = len(shape)
    return pl.BlockSpec(shape, lambda *_: (0,) * nd, pipeline_mode=pl.Buffered(1))


def _row_blocks(b, t, rows):
    if t >= rows:
        assert t % rows == 0, (t, rows)
        return 1, rows
    bb = max(1, min(b, rows // t))
    while b % bb:
        bb -= 1
    return bb, t


def _mod_norm(x, g, shift, scale):
    inv = lax.rsqrt(jnp.mean(x * x, axis=-1, keepdims=True) + EPS)
    return (x * inv) * g * (1.0 + scale) + shift


def _dot(a, b):
    return jnp.dot(a, b, preferred_element_type=F32)


def _log_sigmoid(x):
    return jnp.minimum(x, 0.0) - jnp.log1p(jnp.exp(-jnp.abs(x)))


def _ada_kernel(c_ref, w_ref, b_ref, o_ref):
    c = c_ref[...]
    s = (c * jax.nn.sigmoid(c)).astype(BF16)
    o_ref[...] = _dot(s, w_ref[...].astype(BF16)) + b_ref[...]


def _ada_mod(c, w, b):
    bc, d = c.shape
    n = w.shape[1]
    tn = 1024
    assert n % tn == 0
    return pl.pallas_call(
        _ada_kernel,
        grid=(n // tn,),
        in_specs=[
            pl.BlockSpec((bc, d), lambda j: (0, 0)),
            pl.BlockSpec((d, tn), lambda j: (0, j)),
            pl.BlockSpec((1, tn), lambda j: (0, j)),
        ],
        out_specs=pl.BlockSpec((bc, tn), lambda j: (0, j)),
        out_shape=jax.ShapeDtypeStruct((bc, n), F32),
        compiler_params=_params("parallel"),
        name="ada_mod",
    )(c, w, b.reshape(1, n))


def _ffn_kernel(x_ref, mod_ref, *rest, base, merge, final):
    rest = list(rest)
    x = x_ref[...]
    bb, tm, d = x.shape
    if merge:
        a_ref, t_ref, ga_ref, gb_ref, wr_ref, wa_ref, wo_ref = rest[:7]
        rest = rest[7:]
        rows = bb * tm
        pr = _dot(a_ref[...].reshape(rows, -1), wr_ref[...])
        pa = _dot(t_ref[...].reshape(rows, -1), wa_ref[...])
        z = (jax.nn.sigmoid(ga_ref[...].reshape(rows, d)) * pr
             + jax.nn.sigmoid(gb_ref[...].reshape(rows, d)) * pa)
        x = x + (1.0 + mod_ref[:, 5:6, :]) * _dot(z.astype(BF16), wo_ref[...]).reshape(bb, tm, d)
    g_ref, w1_ref, w3_ref, w2_ref = rest[:4]
    rest = rest[4:]
    if final:
        fg_ref, fmod_ref, o_ref = rest
    else:
        (o_ref,) = rest
    shift = mod_ref[:, base:base + 1, :]
    scale = mod_ref[:, base + 1:base + 2, :]
    gate = 1.0 + mod_ref[:, base + 2:base + 3, :]
    hb = _mod_norm(x, g_ref[...], shift, scale).reshape(bb * tm, d).astype(BF16)
    a = _dot(hb, w1_ref[...])
    b = _dot(hb, w3_ref[...])
    t = ((a * jax.nn.sigmoid(a)) * b).astype(BF16)
    y = _dot(t, w2_ref[...]).reshape(bb, tm, d)
    out = x + MACARON_W * gate * y
    if final:
        out = _mod_norm(out, fg_ref[...], fmod_ref[:, 0:1, :], fmod_ref[:, 1:2, :])
    o_ref[...] = out


def _ffn(x, mod, g, w1, w3, w2, base, merge=None, final=None):
    b, t, d = x.shape
    dff = w1.shape[1]
    bb, tm = _row_blocks(b, t, ROWS_PER_STEP)
    tok = lambda n: pl.BlockSpec((bb, tm, n), lambda i, j: (i, j, 0))
    modspec = lambda n: pl.BlockSpec((bb, n, d), lambda i, j: (i, 0, 0))
    in_specs = [tok(d), modspec(mod.shape[1])]
    args = [x, mod]
    if merge is not None:
        a_out, att, ga, gb, w_br_rnn, w_br_att, w_out = merge
        in_specs += [tok(a_out.shape[-1]), tok(att.shape[-1]), tok(d), tok(d),
                     _resident(w_br_rnn.shape), _resident(w_br_att.shape), _resident(w_out.shape)]
        args += list(merge)
    in_specs += [_resident((1, d)), _resident((d, dff)), _resident((d, dff)), _resident((dff, d))]
    args += [g.reshape(1, d), w1, w3, w2]
    if final is not None:
        fg, fmod = final
        in_specs += [_resident((1, d)), modspec(fmod.shape[1])]
        args += [fg.reshape(1, d), fmod]
    return pl.pallas_call(
        functools.partial(_ffn_kernel, base=base, merge=merge is not None, final=final is not None),
        grid=(b // bb, t // tm),
        in_specs=in_specs,
        out_specs=tok(d),
        out_shape=jax.ShapeDtypeStruct((b, t, d), F32),
        compiler_params=_params("parallel", "parallel"),
        name="ffn" + ("_merge" if merge is not None else "") + ("_final" if final is not None else ""),
    )(*args)


def _inproj_kernel(x_ref, mod_ref, g_ref, wm_ref, wf_ref, wg_ref, bf_ref,
                   xr_o, gr_o, q_o, k_o, v_o, lf_o, ga_o, gb_o, *, q_scale):
    x = x_ref[...]
    bb, tm, d = x.shape
    hb = _mod_norm(x, g_ref[...], mod_ref[:, 3:4, :], mod_ref[:, 4:5, :])
    hb = hb.reshape(bb * tm, d).astype(BF16)

    def proj(w_ref, idx):
        return _dot(hb, w_ref[:, idx * d:(idx + 1) * d]).reshape(bb, tm, d)

    xr_o[...] = proj(wm_ref, 0)
    gr_o[...] = proj(wm_ref, 1)
    q_o[...] = (proj(wm_ref, 2) * q_scale).astype(q_o.dtype)
    k_o[...] = proj(wm_ref, 3)
    v_o[...] = proj(wm_ref, 4)
    ga_o[...] = proj(wg_ref, 0)
    gb_o[...] = proj(wg_ref, 1)
    n_heads = lf_o.shape[-1]
    lf = _log_sigmoid(_dot(hb, wf_ref[...]) + bf_ref[...])
    lf_o[...] = lf[:, :n_heads].reshape(bb, tm, n_heads)


def _inproj(x, mod, g, w_main, w_fgt, w_gate, b_fgt_pad, n_heads, q_scale):
    b, t, d = x.shape
    bb, tm = _row_blocks(b, t, ROWS_PER_STEP)
    tok = pl.BlockSpec((bb, tm, d), lambda i, j: (i, j, 0))
    f32_out = jax.ShapeDtypeStruct((b, t, d), F32)
    return pl.pallas_call(
        functools.partial(_inproj_kernel, q_scale=q_scale),
        grid=(b // bb, t // tm),
        in_specs=[tok, pl.BlockSpec((bb, mod.shape[1], d), lambda i, j: (i, 0, 0)),
                  _resident((1, d)), _resident(w_main.shape), _resident(w_fgt.shape),
                  _resident(w_gate.shape), _resident((1, LANE))],
        out_specs=[tok, tok, tok, tok, tok,
                   pl.BlockSpec((bb, tm, n_heads), lambda i, j: (i, j, 0)), tok, tok],
        out_shape=[f32_out, f32_out, jax.ShapeDtypeStruct((b, t, d), BF16), f32_out, f32_out,
                   jax.ShapeDtypeStruct((b, t, n_heads), F32), f32_out, f32_out],
        compiler_params=_params("parallel", "parallel"),
        name="in_proj",
    )(x, mod, g.reshape(1, d), w_main, w_fgt, w_gate, b_fgt_pad)


def _linear_scan(a, u, h_in):
    t = a.shape[0]
    n = t // SUBLANE
    a3 = a.reshape(n, SUBLANE, LANE)
    u3 = u.reshape(n, SUBLANE, LANE)
    row = lax.broadcasted_iota(jnp.int32, (n, SUBLANE, LANE), 1)
    for dist in (1, 2, 4):
        keep = row >= dist
        u_prev = pltpu.roll(u3, dist, axis=1)
        a_prev = pltpu.roll(a3, dist, axis=1)
        u3 = jnp.where(keep, a3 * u_prev, 0.0) + u3
        a3 = jnp.where(keep, a3 * a_prev, a3)
    h = h_in
    outs = []
    for grp in range(n):
        hg = u3[grp] + a3[grp] * h
        outs.append(hg)
        h = hg[SUBLANE - 1:SUBLANE, :]
    return jnp.concatenate(outs, axis=0)


def _rglru_kernel(xr_ref, gr_ref, conv0_ref, h0_ref, cw_ref, cb_ref, wg_ref, ba_ref, bx_ref, lam_ref,
                  aout_ref, convo_ref, hlast_ref, buf, hcar):
    tc, c = xr_ref.shape[1], xr_ref.shape[2]
    kw = cw_ref.shape[0]
    lo = SUBLANE - (kw - 1)

    @pl.when(pl.program_id(1) == 0)
    def _():
        buf[lo:SUBLANE, :] = conv0_ref[0]
        hcar[...] = h0_ref[0]

    buf[SUBLANE:SUBLANE + tc, :] = xr_ref[0]
    xc = cb_ref[...]
    for j in range(kw):
        xc = xc + buf[lo + j:lo + j + tc, :] * cw_ref[j:j + 1, :]
    tail = buf[SUBLANE + tc - (kw - 1):SUBLANE + tc, :]
    convo_ref[0] = tail
    buf[lo:SUBLANE, :] = tail

    for ch in range(c // LANE):
        sl = slice(ch * LANE, (ch + 1) * LANE)
        xcc = xc[:, sl]
        gates = _dot(xcc.astype(BF16), wg_ref[ch])
        r = jax.nn.sigmoid(gates[:, :LANE] + ba_ref[:, sl])
        i = jax.nn.sigmoid(gates[:, LANE:] + bx_ref[:, sl])
        neg_lam = -lam_ref[:, sl]
        softplus = jnp.maximum(neg_lam, 0.0) + jnp.log1p(jnp.exp(-jnp.abs(neg_lam)))
        log_a = (-RG_C) * r * softplus
        a = jnp.exp(log_a)
        u = jnp.sqrt(-jnp.tanh(log_a) * (a * a + 1.0)) * (i * xcc)
        hs = _linear_scan(a, u, hcar[:, sl])
        hcar[:, sl] = hs[tc - 1:tc, :]
        aout_ref[0, :, sl] = (jax.nn.gelu(gr_ref[0, :, sl]) * hs).astype(aout_ref.dtype)
    hlast_ref[0] = hcar[...]


def _rglru(xr, gr, conv0, h0, conv_w, conv_b, w_gates, ba, bx, lam):
    b, t, c = xr.shape
    kw = conv_w.shape[0]
    tc = min(t, SCAN_ROWS)
    assert t % tc == 0 and tc % SUBLANE == 0 and tc >= kw - 1 and kw - 1 <= SUBLANE
    tok = pl.BlockSpec((1, tc, c), lambda i, j: (i, j, 0))
    per_b = lambda n: pl.BlockSpec((1, n, c), lambda i, j: (i, 0, 0))
    vec = _resident((1, c))
    return pl.pallas_call(
        _rglru_kernel,
        grid=(b, t // tc),
        in_specs=[tok, tok, per_b(kw - 1), per_b(1), _resident((kw, c)), vec,
                  _resident(w_gates.shape), vec, vec, vec],
        out_specs=[tok, per_b(kw - 1), per_b(1)],
        out_shape=[jax.ShapeDtypeStruct((b, t, c), BF16),
                   jax.ShapeDtypeStruct((b, kw - 1, c), F32),
                   jax.ShapeDtypeStruct((b, 1, c), F32)],
        scratch_shapes=[pltpu.VMEM((SUBLANE + tc, c), F32), pltpu.VMEM((1, c), F32)],
        compiler_params=_params("parallel", "arbitrary"),
        name="rglru",
    )(xr, gr, conv0, h0.reshape(b, 1, c), conv_w, conv_b.reshape(1, c), w_gates,
      ba.reshape(1, c), bx.reshape(1, c), lam.reshape(1, c))


def _cumsum_kernel(x_ref, o_ref):
    h, t = x_ref.shape[1], x_ref.shape[2]
    r = lax.broadcasted_iota(jnp.int32, (CUMSUM_CHUNK, CUMSUM_CHUNK), 0)
    c = lax.broadcasted_iota(jnp.int32, (CUMSUM_CHUNK, CUMSUM_CHUNK), 1)
    upper = (r <= c).astype(F32)
    carry = jnp.zeros((h, 1), F32)
    for i in range(t // CUMSUM_CHUNK):
        sl = slice(i * CUMSUM_CHUNK, (i + 1) * CUMSUM_CHUNK)
        cs = jnp.dot(x_ref[0, :, sl], upper, preferred_element_type=F32,
                     precision=lax.Precision.HIGHEST) + carry
        o_ref[0, :, sl] = cs
        carry = cs[:, CUMSUM_CHUNK - 1:CUMSUM_CHUNK]


def _cumsum_lanes(x):
    b, h, t = x.shape
    assert t % CUMSUM_CHUNK == 0
    spec = pl.BlockSpec((1, h, t), lambda i: (i, 0, 0))
    return pl.pallas_call(
        _cumsum_kernel, grid=(b,), in_specs=[spec], out_specs=spec,
        out_shape=jax.ShapeDtypeStruct((b, h, t), F32),
        compiler_params=_params("parallel"), name="fgt_cumsum",
    )(x)


def _head_column(f_blk, head):
    lane = lax.broadcasted_iota(jnp.int32, f_blk.shape, 1)
    return jnp.sum(jnp.where(lane == head, f_blk, 0.0), axis=-1, keepdims=True)


def _softmax_step(qm, kt, vt, bias, carry, mask=None):
    m_i, l_i, acc = carry
    s = lax.dot_general(qm, kt, (((1,), (1,)), ((), ())), preferred_element_type=F32) + bias
    if mask is not None:
        s = jnp.where(mask, s, -jnp.inf)
    m_new = jnp.maximum(m_i, jnp.max(s, axis=-1, keepdims=True))
    alpha = jnp.exp2(m_i - m_new)
    p = jnp.exp2(s - m_new)
    l_new = alpha * l_i + jnp.sum(p, axis=-1, keepdims=True)
    acc = alpha * acc + _dot(p.astype(BF16), vt)
    return m_new, l_new, acc


def _split3(x):
    hi = x.astype(BF16).astype(F32)
    rest = x - hi
    mid = rest.astype(BF16).astype(F32)
    lo = (rest - mid).astype(BF16).astype(F32)
    return hi, mid, lo


def _bias_lanes(f_rows):
    heads, t = f_rows.shape
    n = 3 * heads
    terms = _split3(f_rows)
    rows = 2 * SUBLANE
    assert 2 * n <= rows
    row = lax.broadcasted_iota(jnp.int32, (rows, t), 0)
    packed = jnp.zeros((rows, t), F32)
    for hh in range(heads):
        for j in range(3):
            term = terms[j][hh:hh + 1, :]
            packed = jnp.where((row == 3 * hh + j) | (row == n + 3 * hh + j), term, packed)
    cols = jnp.concatenate([packed, jnp.zeros((LANE - rows, t), F32)], axis=0).T
    lane = lax.broadcasted_iota(jnp.int32, (1, LANE), 1)
    key_lanes = jnp.where(lane < n, -cols, jnp.where(lane < 2 * n, 1.0, 0.0))
    query_lanes = jnp.where(lane < n, 1.0, jnp.where(lane < 2 * n, cols, 0.0))
    return key_lanes, query_lanes


def _qk_scores(kt, queries):
    return [lax.dot_general(kt, qa, (((1,), (1,)), ((), ())), preferred_element_type=F32)
            for qa in queries]


def _softmax_pv(scores, values, states, masks):
    partial = []
    for s, (m_i, l_i, acc), mask in zip(scores, states, masks):
        if mask is not None:
            s = jnp.where(mask, s, -jnp.inf)
        m_new = jnp.maximum(m_i, jnp.max(s, axis=0, keepdims=True))
        alpha = jnp.exp2(m_i - m_new)
        p = jnp.exp2(s - m_new)
        l_new = alpha * l_i + jnp.sum(p, axis=0, keepdims=True)
        partial.append((m_new, l_new, alpha, p.astype(BF16)))
    return [(m_new, l_new, alpha * acc + _dot(vt, p))
            for (m_new, l_new, alpha, p), (_, _, acc), vt in zip(partial, states, values)]


def _merge_softmax(a, b):
    (m_a, l_a, acc_a), (m_b, l_b, acc_b) = a, b
    m = jnp.maximum(m_a, m_b)
    w_a = jnp.exp2(m_a - m)
    w_b = jnp.exp2(m_b - m)
    return m, w_a * l_a + w_b * l_b, w_a * acc_a + w_b * acc_b


def _attn_prompt_kernel(q_ref, k_ref, v_ref, f_ref, o_ref, kaug, qext, vt_s, s_buf, *, head_dim):
    t = q_ref.shape[1]
    tq = ATT_TILE
    heads = LANE // head_dim
    lane = lax.broadcasted_iota(jnp.int32, (1, LANE), 1)
    sub = lax.broadcasted_iota(jnp.int32, (LANE, 1), 0)
    key_pos = lax.broadcasted_iota(jnp.int32, (tq, tq), 0)
    qry_pos = lax.broadcasted_iota(jnp.int32, (tq, tq), 1)
    causal = key_pos <= qry_pos

    key_lanes, query_lanes = _bias_lanes(f_ref[0, 0] * LOG2_E)
    kaug[:, :LANE] = k_ref[0].astype(BF16)
    kaug[:, LANE:] = key_lanes.astype(BF16)
    qext[...] = query_lanes.astype(BF16)
    vt_s[...] = v_ref[0].T.astype(BF16)

    def query_tile(q0):
        q = q_ref[0, pl.ds(q0, tq), :]
        extra = qext[pl.ds(q0, tq), :]
        out = []
        for hh in range(heads):
            in_head = (lane >= hh * head_dim) & (lane < (hh + 1) * head_dim)
            lo = 3 * hh
            hi = 3 * heads + 3 * hh
            own = ((lane >= lo) & (lane < lo + 3)) | ((lane >= hi) & (lane < hi + 3))
            out.append(jnp.concatenate([jnp.where(in_head, q, jnp.zeros_like(q)),
                                        jnp.where(own, extra, jnp.zeros_like(extra))], axis=1))
        return out

    def init_state():
        return (jnp.full((1, tq), NEG_BIG, F32), jnp.zeros((1, tq), F32), jnp.zeros((LANE, tq), F32))

    def finish(q0, states):
        out_t = jnp.zeros((LANE, tq), F32)
        for hh, (_, l_i, acc) in enumerate(states):
            in_head = (sub >= hh * head_dim) & (sub < (hh + 1) * head_dim)
            out_t = jnp.where(in_head, acc / l_i, out_t)
        o_ref[0, pl.ds(q0, tq), :] = out_t.T.astype(o_ref.dtype)

    def keys(k0):
        return kaug[pl.ds(k0, tq), :]

    def values(k0):
        return vt_s[:, pl.ds(k0, tq)]

    n_chain = 2 * heads
    unmasked = [None] * n_chain

    def pair_queries(a):
        return query_tile(2 * a * tq) + query_tile((2 * a + 1) * tq)

    def put_scores(slot, k0, queries):
        for c, s in enumerate(_qk_scores(keys(k0), queries)):
            s_buf[slot, c] = s

    def get_scores(slot):
        return [s_buf[slot, c] for c in range(n_chain)]

    n_pairs = t // (2 * tq)
    queries = pair_queries(0)
    put_scores(0, 0, queries)
    for a in range(n_pairs):
        cur, oth = a % 2, 1 - a % 2

        def kv_tiles(i, states, queries=queries, cur=cur, oth=oth):
            k0 = pl.multiple_of(2 * i * tq, 2 * tq)
            k1 = pl.multiple_of(k0 + tq, tq)
            put_scores(oth, k1, queries)
            states = _softmax_pv(get_scores(cur), [values(k0)] * n_chain, states, unmasked)
            put_scores(cur, pl.multiple_of(k1 + tq, tq), queries)
            return _softmax_pv(get_scores(oth), [values(k1)] * n_chain, states, unmasked)

        states = lax.fori_loop(0, a, kv_tiles, [init_state() for _ in range(n_chain)])
        q_lo, q_hi = 2 * a * tq, (2 * a + 1) * tq
        last = _qk_scores(keys(q_hi), queries[heads:])
        if a + 1 < n_pairs:
            queries = pair_queries(a + 1)
            put_scores(oth, 0, queries)
        states = _softmax_pv(
            get_scores(cur) + last,
            [values(q_lo)] * n_chain + [values(q_hi)] * heads,
            states + [init_state() for _ in range(heads)],
            [causal] * heads + [None] * heads + [causal] * heads)
        finish(q_lo, states[:heads])
        finish(q_hi, [_merge_softmax(states[heads + hh], states[n_chain + hh]) for hh in range(heads)])


def _attn_prompt(qs, k, v, f_row, head_dim):
    b, t, w = qs.shape
    h = f_row.shape[1]
    heads = LANE // head_dim
    assert LANE % head_dim == 0 and w % LANE == 0 and t % (2 * ATT_TILE) == 0 and h % heads == 0
    tok = pl.BlockSpec((1, t, LANE), lambda i, j: (i, 0, j))
    return pl.pallas_call(
        functools.partial(_attn_prompt_kernel, head_dim=head_dim),
        grid=(b, w // LANE),
        in_specs=[tok, tok, tok, pl.BlockSpec((1, 1, heads, t), lambda i, j: (i, j, 0, 0))],
        out_specs=tok,
        out_shape=jax.ShapeDtypeStruct((b, t, w), BF16),
        scratch_shapes=[pltpu.VMEM((t, 2 * LANE), BF16), pltpu.VMEM((t, LANE), BF16),
                        pltpu.VMEM((LANE, t), BF16),
                        pltpu.VMEM((2, 2 * heads, ATT_TILE, ATT_TILE), F32)],
        compiler_params=_params("parallel", "arbitrary"),
        name="attn_prompt",
    )(qs, k, v, f_row.reshape(b, h // heads, heads, t))


def _attn_sample_kernel(q_ref, kn_ref, vn_ref, ck_ref, cv_ref, fn_ref, fnt_ref, fct_ref, o_ref,
                        kall, vall, *, head_dim):
    ts, past = q_ref.shape[1], ck_ref.shape[1]
    heads_per_block = LANE // head_dim
    pair = pl.program_id(1)
    kall[0:past, :] = ck_ref[0].astype(BF16)
    vall[0:past, :] = cv_ref[0].astype(BF16)
    kall[past:, :] = jnp.zeros((LANE, LANE), BF16)
    vall[past:, :] = jnp.zeros((LANE, LANE), BF16)
    kall[past:past + ts, :] = kn_ref[0].astype(BF16)
    vall[past:past + ts, :] = vn_ref[0].astype(BF16)
    lane = lax.broadcasted_iota(jnp.int32, (1, LANE), 1)
    row = lax.broadcasted_iota(jnp.int32, (ts, past + LANE), 0)
    col = lax.broadcasted_iota(jnp.int32, (ts, past + LANE), 1)
    visible = col <= past + row
    q = q_ref[0]
    out = jnp.zeros((ts, LANE), F32)
    for hh in range(heads_per_block):
        head = pair * heads_per_block + hh
        in_head = (lane >= hh * head_dim) & (lane < (hh + 1) * head_dim)
        qm = jnp.where(in_head, q, jnp.zeros_like(q))
        f_cache = fct_ref[0, pl.ds(head, 1), :]
        total = f_cache[:, past - 1:past]
        f_keys = jnp.concatenate([f_cache, total + fnt_ref[0, pl.ds(head, 1), :]], axis=1)
        bias = ((total + _head_column(fn_ref[0], head)) - f_keys) * LOG2_E
        init = (jnp.full((ts, 1), NEG_BIG, F32), jnp.zeros((ts, 1), F32), jnp.zeros((ts, LANE), F32))
        _, l_i, acc = _softmax_step(qm, kall[...], vall[...], bias, init, mask=visible)
        out = jnp.where(in_head, acc / l_i, out)
    o_ref[0] = out.astype(o_ref.dtype)


def _attn_sample(qs, k_new, v_new, cache_k, cache_v, fn_col, fn_row, fc_row, head_dim):
    b, ts, w = qs.shape
    past = cache_k.shape[1]
    h = fn_col.shape[-1]
    assert ts <= LANE and past % LANE == 0
    new = pl.BlockSpec((1, ts, LANE), lambda i, j: (i, 0, j))
    old = pl.BlockSpec((1, past, LANE), lambda i, j: (i, 0, j))
    return pl.pallas_call(
        functools.partial(_attn_sample_kernel, head_dim=head_dim),
        grid=(b, w // LANE),
        in_specs=[new, new, new, old, old,
                  pl.BlockSpec((1, ts, h), lambda i, j: (i, 0, 0)),
                  pl.BlockSpec((1, h, LANE), lambda i, j: (i, 0, 0)),
                  pl.BlockSpec((1, h, past), lambda i, j: (i, 0, 0))],
        out_specs=new,
        out_shape=jax.ShapeDtypeStruct((b, ts, w), BF16),
        scratch_shapes=[pltpu.VMEM((past + LANE, LANE), BF16), pltpu.VMEM((past + LANE, LANE), BF16)],
        compiler_params=_params("parallel", "arbitrary"),
        name="attn_sample",
    )(qs, k_new, v_new, cache_k, cache_v, fn_col, fn_row, fc_row)


def _gate_weights(wa, wx):
    nb, bw, _ = wa.shape
    per = LANE // bw

    def chunked(w):
        w = w.reshape(nb // per, per, bw, bw)
        eye = jnp.eye(per, dtype=w.dtype)
        return jnp.einsum("cpij,pq->cpiqj", w, eye).reshape(nb // per, LANE, LANE)

    return jnp.concatenate([chunked(wa), chunked(wx)], axis=-1).astype(BF16)


def _pad_lanes(x, n):
    return jnp.pad(x, [(0, 0)] * (x.ndim - 1) + [(0, n - x.shape[-1])])


def _mixer(x, mod, w, conv0, h0, attend):
    b, t, d = x.shape
    xr, gr, qs, k, v, logf, ga, gb = _inproj(
        x, mod, w["mix_norm"], w["w_main"], w["w_fgt"], w["w_gate"], w["b_fgt"],
        w["n_heads"], w["q_scale"])
    a_out, conv_state, h_last = _rglru(xr, gr, conv0, h0, w["conv_w"], w["conv_b"], w["w_gates"],
                                       w["rg_ba"], w["rg_bx"], w["rg_lam"])
    att = attend(qs, k, v, logf)
    merge = (a_out, att, ga, gb, w["w_br_rnn"], w["w_br_att"], w["w_out"])
    return merge, k, v, logf, conv_state, h_last.reshape(b, -1)


def kernel(x_prompt, x_sample, cache_k, cache_v, cache_logf, state_conv, state_rglru, c_prompt, c_sample, w_ada, b_ada, ffn1_norm, ffn1_w1, ffn1_w3, ffn1_w2, mix_norm, w_in, b_fgt, conv_w, conv_b, rg_wa, rg_ba, rg_wx, rg_bx, rg_lam, w_br_rnn, w_br_att, w_out, ffn2_norm, ffn2_w1, ffn2_w3, ffn2_w2, final_norm, w_ada_f, b_ada_f):
    bp, seq, d = x_prompt.shape
    bs, dec_seq, _ = x_sample.shape
    depth = w_ada.shape[0]
    n_heads = b_fgt.shape[-1]
    head_dim = cache_k.shape[-1]
    d_rnn = conv_w.shape[-1]
    kw = conv_w.shape[1]
    att_w = n_heads * head_dim
    past = cache_k.shape[2]
    assert d_rnn == d and att_w == d and n_heads <= LANE
    assert w_in.shape[-1] == 2 * d_rnn + 3 * att_w + n_heads + 2 * d

    c_all = jnp.concatenate([c_prompt, c_sample], axis=0)
    mod_f = _ada_mod(c_all, w_ada_f, b_ada_f).reshape(bp + bs, 2, d)
    xp, xs = x_prompt, x_sample
    outs_p, outs_s = [], []
    for l in range(depth):
        mod = _ada_mod(c_all, w_ada[l], b_ada[l]).reshape(bp + bs, N_MOD, d)
        mp, ms = mod[:bp], mod[bp:]
        n_main = 2 * d_rnn + 3 * att_w
        w = {
            "mix_norm": mix_norm[l], "n_heads": n_heads, "q_scale": head_dim ** -0.5 * LOG2_E,
            "w_main": w_in[l, :, :n_main].astype(BF16),
            "w_fgt": _pad_lanes(w_in[l, :, n_main:n_main + n_heads], LANE).astype(BF16),
            "w_gate": w_in[l, :, n_main + n_heads:].astype(BF16),
            "b_fgt": _pad_lanes(b_fgt[l].reshape(1, n_heads), LANE),
            "conv_w": conv_w[l], "conv_b": conv_b[l], "w_gates": _gate_weights(rg_wa[l], rg_wx[l]),
            "rg_ba": rg_ba[l], "rg_bx": rg_bx[l], "rg_lam": rg_lam[l],
            "w_br_rnn": w_br_rnn[l].astype(BF16), "w_br_att": w_br_att[l].astype(BF16),
            "w_out": w_out[l].astype(BF16),
        }
        f1 = (ffn1_norm[l], ffn1_w1[l].astype(BF16), ffn1_w3[l].astype(BF16), ffn1_w2[l].astype(BF16))
        f2 = (ffn2_norm[l], ffn2_w1[l].astype(BF16), ffn2_w3[l].astype(BF16), ffn2_w2[l].astype(BF16))
        last = l == depth - 1

        xp = _ffn(xp, mp, *f1, base=0)
        xs = _ffn(xs, ms, *f1, base=0)

        def attend_prompt(qs, k, v, logf):
            return _attn_prompt(qs, k, v, _cumsum_lanes(jnp.swapaxes(logf, 1, 2)), head_dim)

        def attend_sample(qs, k, v, logf):
            fc_row = _cumsum_lanes(jnp.swapaxes(cache_logf[l], 1, 2))
            fn_row = _cumsum_lanes(_pad_lanes(jnp.swapaxes(logf, 1, 2), LANE))
            fn_col = jnp.swapaxes(fn_row[:, :, :dec_seq], 1, 2)
            return _attn_sample(qs, k, v, cache_k[l].reshape(bs, past, att_w),
                                cache_v[l].reshape(bs, past, att_w), fn_col, fn_row, fc_row, head_dim)

        merge_p, kp, vp, lp, cp, hp = _mixer(xp, mp, w, jnp.zeros((bp, kw - 1, d_rnn), F32),
                                             jnp.zeros((bp, d_rnn), F32), attend_prompt)
        merge_s, ks, vs, ls, cs, hs = _mixer(xs, ms, w, state_conv[l], state_rglru[l], attend_sample)

        xp = _ffn(xp, mp, *f2, base=6, merge=merge_p,
                  final=(final_norm, mod_f[:bp]) if last else None)
        xs = _ffn(xs, ms, *f2, base=6, merge=merge_s,
                  final=(final_norm, mod_f[bp:]) if last else None)
        heads_p = (bp, seq, n_heads, head_dim)
        heads_s = (bs, dec_seq, n_heads, head_dim)
        outs_p.append((kp.reshape(heads_p), vp.reshape(heads_p), lp, cp, hp))
        outs_s.append((ks.reshape(heads_s), vs.reshape(heads_s), ls, cs, hs))
    stack = lambda outs: tuple(jnp.stack(leaf) for leaf in zip(*outs))
    return (xp, xs) + stack(outs_p) + stack(outs_s)
```

```python
import functools
import math

import jax
import jax.numpy as jnp
from jax import lax
from jax.experimental import pallas as pl
from jax.experimental.pallas import tpu as pltpu

F32 = jnp.float32
BF16 = jnp.bfloat16

LANE = 128
SUBLANE = 8
VMEM_BYTES_V7X = 64 * 1024 * 1024
VMEM_LIMIT = VMEM_BYTES_V7X - 8 * 1024 * 1024

EPS = 1e-6
MACARON_W = 0.5
RG_C = 8.0
N_MOD = 9

ROWS_PER_STEP = 512
SCAN_ROWS = 256
ATT_TILE = 256
CUMSUM_CHUNK = 128
NEG_BIG = -1e30
LOG2_E = math.log2(math.e)


def _params(*sem):
    return pltpu.CompilerParams(dimension_semantics=sem, vmem_limit_bytes=VMEM_LIMIT)


def _resident(shape):
    nd = len(shape)
    return pl.BlockSpec(shape, lambda *_: (0,) * nd, pipeline_mode=pl.Buffered(1))


def _row_blocks(b, t, rows):
    if t >= rows:
        assert t % rows == 0, (t, rows)
        return 1, rows
    bb = max(1, min(b, rows // t))
    while b % bb:
        bb -= 1
    return bb, t


def _mod_norm(x, g, shift, scale):
    inv = lax.rsqrt(jnp.mean(x * x, axis=-1, keepdims=True) + EPS)
    return (x * inv) * g * (1.0 + scale) + shift


def _dot(a, b):
    return jnp.dot(a, b, preferred_element_type=F32)


def _log_sigmoid(x):
    return jnp.minimum(x, 0.0) - jnp.log1p(jnp.exp(-jnp.abs(x)))


def _ada_kernel(c_ref, w_ref, b_ref, o_ref):
    c = c_ref[...]
    s = (c * jax.nn.sigmoid(c)).astype(BF16)
    o_ref[...] = _dot(s, w_ref[...].astype(BF16)) + b_ref[...]


def _ada_mod(c, w, b):
    bc, d = c.shape
    n = w.shape[1]
    tn = 1024
    assert n % tn == 0
    return pl.pallas_call(
        _ada_kernel,
        grid=(n // tn,),
        in_specs=[
            pl.BlockSpec((bc, d), lambda j: (0, 0)),
            pl.BlockSpec((d, tn), lambda j: (0, j)),
            pl.BlockSpec((1, tn), lambda j: (0, j)),
        ],
        out_specs=pl.BlockSpec((bc, tn), lambda j: (0, j)),
        out_shape=jax.ShapeDtypeStruct((bc, n), F32),
        compiler_params=_params("parallel"),
        name="ada_mod",
    )(c, w, b.reshape(1, n))


def _ffn_kernel(x_ref, mod_ref, *rest, base, merge, final):
    rest = list(rest)
    x = x_ref[...]
    bb, tm, d = x.shape
    if merge:
        a_ref, t_ref, ga_ref, gb_ref, wr_ref, wa_ref, wo_ref = rest[:7]
        rest = rest[7:]
        rows = bb * tm
        pr = _dot(a_ref[...].reshape(rows, -1), wr_ref[...])
        pa = _dot(t_ref[...].reshape(rows, -1), wa_ref[...])
        z = (jax.nn.sigmoid(ga_ref[...].reshape(rows, d)) * pr
             + jax.nn.sigmoid(gb_ref[...].reshape(rows, d)) * pa)
        x = x + (1.0 + mod_ref[:, 5:6, :]) * _dot(z.astype(BF16), wo_ref[...]).reshape(bb, tm, d)
    g_ref, w1_ref, w3_ref, w2_ref = rest[:4]
    rest = rest[4:]
    if final:
        fg_ref, fmod_ref, o_ref = rest
    else:
        (o_ref,) = rest
    shift = mod_ref[:, base:base + 1, :]
    scale = mod_ref[:, base + 1:base + 2, :]
    gate = 1.0 + mod_ref[:, base + 2:base + 3, :]
    hb = _mod_norm(x, g_ref[...], shift, scale).reshape(bb * tm, d).astype(BF16)
    a = _dot(hb, w1_ref[...])
    b = _dot(hb, w3_ref[...])
    t = ((a * jax.nn.sigmoid(a)) * b).astype(BF16)
    y = _dot(t, w2_ref[...]).reshape(bb, tm, d)
    out = x + MACARON_W * gate * y
    if final:
        out = _mod_norm(out, fg_ref[...], fmod_ref[:, 0:1, :], fmod_ref[:, 1:2, :])
    o_ref[...] = out


def _ffn(x, mod, g, w1, w3, w2, base, merge=None, final=None):
    b, t, d = x.shape
    dff = w1.shape[1]
    bb, tm = _row_blocks(b, t, ROWS_PER_STEP)
    tok = lambda n: pl.BlockSpec((bb, tm, n), lambda i, j: (i, j, 0))
    modspec = lambda n: pl.BlockSpec((bb, n, d), lambda i, j: (i, 0, 0))
    in_specs = [tok(d), modspec(mod.shape[1])]
    args = [x, mod]
    if merge is not None:
        a_out, att, ga, gb, w_br_rnn, w_br_att, w_out = merge
        in_specs += [tok(a_out.shape[-1]), tok(att.shape[-1]), tok(d), tok(d),
                     _resident(w_br_rnn.shape), _resident(w_br_att.shape), _resident(w_out.shape)]
        args += list(merge)
    in_specs += [_resident((1, d)), _resident((d, dff)), _resident((d, dff)), _resident((dff, d))]
    args += [g.reshape(1, d), w1, w3, w2]
    if final is not None:
        fg, fmod = final
        in_specs += [_resident((1, d)), modspec(fmod.shape[1])]
        args += [fg.reshape(1, d), fmod]
    return pl.pallas_call(
        functools.partial(_ffn_kernel, base=base, merge=merge is not None, final=final is not None),
        grid=(b // bb, t // tm),
        in_specs=in_specs,
        out_specs=tok(d),
        out_shape=jax.ShapeDtypeStruct((b, t, d), F32),
        compiler_params=_params("parallel", "parallel"),
        name="ffn" + ("_merge" if merge is not None else "") + ("_final" if final is not None else ""),
    )(*args)


def _inproj_kernel(x_ref, mod_ref, g_ref, wm_ref, wf_ref, wg_ref, bf_ref,
                   xr_o, gr_o, q_o, k_o, v_o, lf_o, ga_o, gb_o, *, q_scale):
    x = x_ref[...]
    bb, tm, d = x.shape
    hb = _mod_norm(x, g_ref[...], mod_ref[:, 3:4, :], mod_ref[:, 4:5, :])
    hb = hb.reshape(bb * tm, d).astype(BF16)

    def proj(w_ref, idx):
        return _dot(hb, w_ref[:, idx * d:(idx + 1) * d]).reshape(bb, tm, d)

    xr_o[...] = proj(wm_ref, 0)
    gr_o[...] = proj(wm_ref, 1)
    q_o[...] = (proj(wm_ref, 2) * q_scale).astype(q_o.dtype)
    k_o[...] = proj(wm_ref, 3)
    v_o[...] = proj(wm_ref, 4)
    ga_o[...] = proj(wg_ref, 0)
    gb_o[...] = proj(wg_ref, 1)
    n_heads = lf_o.shape[-1]
    lf = _log_sigmoid(_dot(hb, wf_ref[...]) + bf_ref[...])
    lf_o[...] = lf[:, :n_heads].reshape(bb, tm, n_heads)


def _inproj(x, mod, g, w_main, w_fgt, w_gate, b_fgt_pad, n_heads, q_scale):
    b, t, d = x.shape
    bb, tm = _row_blocks(b, t, ROWS_PER_STEP)
    tok = pl.BlockSpec((bb, tm, d), lambda i, j: (i, j, 0))
    f32_out = jax.ShapeDtypeStruct((b, t, d), F32)
    return pl.pallas_call(
        functools.partial(_inproj_kernel, q_scale=q_scale),
        grid=(b // bb, t // tm),
        in_specs=[tok, pl.BlockSpec((bb, mod.shape[1], d), lambda i, j: (i, 0, 0)),
                  _resident((1, d)), _resident(w_main.shape), _resident(w_fgt.shape),
                  _resident(w_gate.shape), _resident((1, LANE))],
        out_specs=[tok, tok, tok, tok, tok,
                   pl.BlockSpec((bb, tm, n_heads), lambda i, j: (i, j, 0)), tok, tok],
        out_shape=[f32_out, f32_out, jax.ShapeDtypeStruct((b, t, d), BF16), f32_out, f32_out,
                   jax.ShapeDtypeStruct((b, t, n_heads), F32), f32_out, f32_out],
        compiler_params=_params("parallel", "parallel"),
        name="in_proj",
    )(x, mod, g.reshape(1, d), w_main, w_fgt, w_gate, b_fgt_pad)


def _linear_scan(a, u, h_in):
    t = a.shape[0]
    n = t // SUBLANE
    a3 = a.reshape(n, SUBLANE, LANE)
    u3 = u.reshape(n, SUBLANE, LANE)
    row = lax.broadcasted_iota(jnp.int32, (n, SUBLANE, LANE), 1)
    for dist in (1, 2, 4):
        keep = row >= dist
        u_prev = pltpu.roll(u3, dist, axis=1)
        a_prev = pltpu.roll(a3, dist, axis=1)
        u3 = jnp.where(keep, a3 * u_prev, 0.0) + u3
        a3 = jnp.where(keep, a3 * a_prev, a3)
    h = h_in
    outs = []
    for grp in range(n):
        hg = u3[grp] + a3[grp] * h
        outs.append(hg)
        h = hg[SUBLANE - 1:SUBLANE, :]
    return jnp.concatenate(outs, axis=0)


def _lru_coefficients(xc, wg_ref, ba_ref, bx_ref, lam_ref):
    a_parts, u_parts = [], []
    for ch in range(xc.shape[1] // LANE):
        sl = slice(ch * LANE, (ch + 1) * LANE)
        xcc = xc[:, sl]
        gates = _dot(xcc.astype(BF16), wg_ref[ch])
        r = jax.nn.sigmoid(gates[:, :LANE] + ba_ref[:, sl])
        i = jax.nn.sigmoid(gates[:, LANE:] + bx_ref[:, sl])
        neg_lam = -lam_ref[:, sl]
        softplus = jnp.maximum(neg_lam, 0.0) + jnp.log1p(jnp.exp(-jnp.abs(neg_lam)))
        log_a = (-RG_C) * r * softplus
        a = jnp.exp(log_a)
        a_parts.append(a)
        u_parts.append(jnp.sqrt(-jnp.tanh(log_a) * (a * a + 1.0)) * (i * xcc))
    return a_parts, u_parts


def _rglru_rows(xr_ref, gr_ref, cw_ref, cb_ref, coeff_refs, aout_ref, buf, hcar):
    tc = xr_ref.shape[1]
    kw = cw_ref.shape[0]
    lo = SUBLANE - (kw - 1)
    buf[SUBLANE:SUBLANE + tc, :] = xr_ref[0]
    xc = cb_ref[...]
    for j in range(kw):
        xc = xc + buf[lo + j:lo + j + tc, :] * cw_ref[j:j + 1, :]
    buf[lo:SUBLANE, :] = buf[SUBLANE + tc - (kw - 1):SUBLANE + tc, :]
    a_parts, u_parts = _lru_coefficients(xc, *coeff_refs)
    for ch, (a, u) in enumerate(zip(a_parts, u_parts)):
        sl = slice(ch * LANE, (ch + 1) * LANE)
        hs = _linear_scan(a, u, hcar[:, sl])
        hcar[:, sl] = hs[tc - 1:tc, :]
        aout_ref[0, :, sl] = (jax.nn.gelu(gr_ref[0, :, sl]) * hs).astype(aout_ref.dtype)


def _rglru_segments(xr_ref, gr_ref, cw_ref, cb_ref, coeff_refs, aout_ref, buf, hcar, xs, hs_s):
    tc, c = xr_ref.shape[1], xr_ref.shape[2]
    kw = cw_ref.shape[0]
    seg = tc // SUBLANE
    n_ch = c // LANE
    sub = lax.broadcasted_iota(jnp.int32, (SUBLANE, c), 0)

    for s in range(SUBLANE):
        for ch in range(n_ch):
            xs[ch, pl.ds(s, seg, stride=SUBLANE), :] = xr_ref[0, s * seg:(s + 1) * seg,
                                                             ch * LANE:(ch + 1) * LANE]
    x = jnp.concatenate([xs[ch] for ch in range(n_ch)], axis=1)

    wrapped = []
    for i in range(kw - 1):
        p = seg - (kw - 1) + i
        cur = x[p * SUBLANE:(p + 1) * SUBLANE, :]
        prev = buf[i * SUBLANE:(i + 1) * SUBLANE, :]
        wrapped.append(pltpu.roll(jnp.where(sub == SUBLANE - 1, prev, cur), 1, axis=0))
    buf[...] = x[(seg - (kw - 1)) * SUBLANE:, :]
    xc = cb_ref[...] + x * cw_ref[kw - 1:kw, :]
    for j in range(1, kw):
        shifted = jnp.concatenate(wrapped[kw - 1 - j:] + [x[:(seg - j) * SUBLANE, :]], axis=0)
        xc = xc + shifted * cw_ref[kw - 1 - j:kw - j, :]

    a_parts, u_parts = _lru_coefficients(xc, *coeff_refs)
    a = jnp.concatenate(a_parts, axis=1)
    u = jnp.concatenate(u_parts, axis=1)

    h = u[:SUBLANE, :]
    prod = a[:SUBLANE, :]
    local, prods = [h], [prod]
    for p in range(1, seg):
        a_p = a[p * SUBLANE:(p + 1) * SUBLANE, :]
        h = a_p * h + u[p * SUBLANE:(p + 1) * SUBLANE, :]
        prod = a_p * prod
        local.append(h)
        prods.append(prod)
    state = hcar[...]
    entering = []
    for s in range(SUBLANE):
        entering.append(state)
        state = h[s:s + 1, :] + prod[s:s + 1, :] * state
    hcar[...] = state
    h_in = jnp.concatenate(entering, axis=0)
    for p in range(seg):
        full = local[p] + prods[p] * h_in
        for ch in range(n_ch):
            hs_s[ch, p * SUBLANE:(p + 1) * SUBLANE, :] = full[:, ch * LANE:(ch + 1) * LANE]

    for ch in range(n_ch):
        sl = slice(ch * LANE, (ch + 1) * LANE)
        hn = jnp.concatenate([hs_s[ch, pl.ds(s, seg, stride=SUBLANE), :] for s in range(SUBLANE)], axis=0)
        aout_ref[0, :, sl] = (jax.nn.gelu(gr_ref[0, :, sl]) * hn).astype(aout_ref.dtype)


def _rglru_kernel(xr_ref, gr_ref, conv0_ref, h0_ref, cw_ref, cb_ref, wg_ref, ba_ref, bx_ref, lam_ref,
                  aout_ref, convo_ref, hlast_ref, buf, hcar, *stage):
    tc = xr_ref.shape[1]
    kw = cw_ref.shape[0]
    segmented = bool(stage)

    @pl.when(pl.program_id(1) == 0)
    def _():
        hcar[...] = h0_ref[0]
        if segmented:
            buf[...] = jnp.zeros(buf.shape, F32)
            for i in range(kw - 1):
                buf[i * SUBLANE + SUBLANE - 1:(i + 1) * SUBLANE, :] = conv0_ref[0, i:i + 1, :]
        else:
            buf[SUBLANE - (kw - 1):SUBLANE, :] = conv0_ref[0]

    coeff_refs = (wg_ref, ba_ref, bx_ref, lam_ref)
    if segmented:
        _rglru_segments(xr_ref, gr_ref, cw_ref, cb_ref, coeff_refs, aout_ref, buf, hcar, *stage)
    else:
        _rglru_rows(xr_ref, gr_ref, cw_ref, cb_ref, coeff_refs, aout_ref, buf, hcar)
    convo_ref[0] = xr_ref[0, tc - (kw - 1):tc, :]
    hlast_ref[0] = hcar[...]


def _rglru(xr, gr, conv0, h0, conv_w, conv_b, w_gates, ba, bx, lam):
    b, t, c = xr.shape
    kw = conv_w.shape[0]
    tc = min(t, SCAN_ROWS)
    assert t % tc == 0 and tc % SUBLANE == 0 and tc >= kw - 1 and kw - 1 <= SUBLANE
    tok = pl.BlockSpec((1, tc, c), lambda i, j: (i, j, 0))
    per_b = lambda n: pl.BlockSpec((1, n, c), lambda i, j: (i, 0, 0))
    vec = _resident((1, c))
    if tc // SUBLANE >= SUBLANE:
        staging = pltpu.VMEM((c // LANE, tc, LANE), F32)
        scratch = [pltpu.VMEM(((kw - 1) * SUBLANE, c), F32), pltpu.VMEM((1, c), F32), staging, staging]
    else:
        scratch = [pltpu.VMEM((SUBLANE + tc, c), F32), pltpu.VMEM((1, c), F32)]
    return pl.pallas_call(
        _rglru_kernel,
        grid=(b, t // tc),
        in_specs=[tok, tok, per_b(kw - 1), per_b(1), _resident((kw, c)), vec,
                  _resident(w_gates.shape), vec, vec, vec],
        out_specs=[tok, per_b(kw - 1), per_b(1)],
        out_shape=[jax.ShapeDtypeStruct((b, t, c), BF16),
                   jax.ShapeDtypeStruct((b, kw - 1, c), F32),
                   jax.ShapeDtypeStruct((b, 1, c), F32)],
        scratch_shapes=scratch,
        compiler_params=_params("parallel", "arbitrary"),
        name="rglru",
    )(xr, gr, conv0, h0.reshape(b, 1, c), conv_w, conv_b.reshape(1, c), w_gates,
      ba.reshape(1, c), bx.reshape(1, c), lam.reshape(1, c))


def _cumsum_kernel(x_ref, o_ref):
    h, t = x_ref.shape[1], x_ref.shape[2]
    r = lax.broadcasted_iota(jnp.int32, (CUMSUM_CHUNK, CUMSUM_CHUNK), 0)
    c = lax.broadcasted_iota(jnp.int32, (CUMSUM_CHUNK, CUMSUM_CHUNK), 1)
    upper = (r <= c).astype(F32)
    carry = jnp.zeros((h, 1), F32)
    for i in range(t // CUMSUM_CHUNK):
        sl = slice(i * CUMSUM_CHUNK, (i + 1) * CUMSUM_CHUNK)
        cs = jnp.dot(x_ref[0, :, sl], upper, preferred_element_type=F32,
                     precision=lax.Precision.HIGHEST) + carry
        o_ref[0, :, sl] = cs
        carry = cs[:, CUMSUM_CHUNK - 1:CUMSUM_CHUNK]


def _cumsum_lanes(x):
    b, h, t = x.shape
    assert t % CUMSUM_CHUNK == 0
    spec = pl.BlockSpec((1, h, t), lambda i: (i, 0, 0))
    return pl.pallas_call(
        _cumsum_kernel, grid=(b,), in_specs=[spec], out_specs=spec,
        out_shape=jax.ShapeDtypeStruct((b, h, t), F32),
        compiler_params=_params("parallel"), name="fgt_cumsum",
    )(x)


def _head_column(f_blk, head):
    lane = lax.broadcasted_iota(jnp.int32, f_blk.shape, 1)
    return jnp.sum(jnp.where(lane == head, f_blk, 0.0), axis=-1, keepdims=True)


def _softmax_step(qm, kt, vt, bias, carry, mask=None):
    m_i, l_i, acc = carry
    s = lax.dot_general(qm, kt, (((1,), (1,)), ((), ())), preferred_element_type=F32) + bias
    if mask is not None:
        s = jnp.where(mask, s, -jnp.inf)
    m_new = jnp.maximum(m_i, jnp.max(s, axis=-1, keepdims=True))
    alpha = jnp.exp2(m_i - m_new)
    p = jnp.exp2(s - m_new)
    l_new = alpha * l_i + jnp.sum(p, axis=-1, keepdims=True)
    acc = alpha * acc + _dot(p.astype(BF16), vt)
    return m_new, l_new, acc


def _split3(x):
    hi = x.astype(BF16).astype(F32)
    rest = x - hi
    mid = rest.astype(BF16).astype(F32)
    lo = (rest - mid).astype(BF16).astype(F32)
    return hi, mid, lo


def _bias_lanes(f_rows):
    heads, t = f_rows.shape
    n = 3 * heads
    terms = _split3(f_rows)
    rows = 2 * SUBLANE
    assert 2 * n <= rows
    row = lax.broadcasted_iota(jnp.int32, (rows, t), 0)
    packed = jnp.zeros((rows, t), F32)
    for hh in range(heads):
        for j in range(3):
            term = terms[j][hh:hh + 1, :]
            packed = jnp.where((row == 3 * hh + j) | (row == n + 3 * hh + j), term, packed)
    cols = jnp.concatenate([packed, jnp.zeros((LANE - rows, t), F32)], axis=0).T
    lane = lax.broadcasted_iota(jnp.int32, (1, LANE), 1)
    key_lanes = jnp.where(lane < n, -cols, jnp.where(lane < 2 * n, 1.0, 0.0))
    query_lanes = jnp.where(lane < n, 1.0, jnp.where(lane < 2 * n, cols, 0.0))
    return key_lanes, query_lanes


def _qk_scores(kt, queries):
    return [lax.dot_general(kt, qa, (((1,), (1,)), ((), ())), preferred_element_type=F32)
            for qa in queries]


def _softmax_pv(scores, values, states, masks):
    partial = []
    for s, (m_i, l_i, acc), mask in zip(scores, states, masks):
        if mask is not None:
            s = jnp.where(mask, s, -jnp.inf)
        m_new = jnp.maximum(m_i, jnp.max(s, axis=0, keepdims=True))
        alpha = jnp.exp2(m_i - m_new)
        p = jnp.exp2(s - m_new)
        l_new = alpha * l_i + jnp.sum(p, axis=0, keepdims=True)
        partial.append((m_new, l_new, alpha, p.astype(BF16)))
    return [(m_new, l_new, alpha * acc + _dot(vt, p))
            for (m_new, l_new, alpha, p), (_, _, acc), vt in zip(partial, states, values)]


def _merge_softmax(a, b):
    (m_a, l_a, acc_a), (m_b, l_b, acc_b) = a, b
    m = jnp.maximum(m_a, m_b)
    w_a = jnp.exp2(m_a - m)
    w_b = jnp.exp2(m_b - m)
    return m, w_a * l_a + w_b * l_b, w_a * acc_a + w_b * acc_b


def _attn_prompt_kernel(q_ref, k_ref, v_ref, f_ref, o_ref, kaug, qext, vt_s, s_buf, *, head_dim):
    t = q_ref.shape[1]
    tq = ATT_TILE
    heads = LANE // head_dim
    lane = lax.broadcasted_iota(jnp.int32, (1, LANE), 1)
    sub = lax.broadcasted_iota(jnp.int32, (LANE, 1), 0)
    key_pos = lax.broadcasted_iota(jnp.int32, (tq, tq), 0)
    qry_pos = lax.broadcasted_iota(jnp.int32, (tq, tq), 1)
    causal = key_pos <= qry_pos

    key_lanes, query_lanes = _bias_lanes(f_ref[0, 0] * LOG2_E)
    kaug[:, :LANE] = k_ref[0].astype(BF16)
    kaug[:, LANE:] = key_lanes.astype(BF16)
    qext[...] = query_lanes.astype(BF16)
    vt_s[...] = v_ref[0].T.astype(BF16)

    def query_tile(q0):
        q = q_ref[0, pl.ds(q0, tq), :]
        extra = qext[pl.ds(q0, tq), :]
        out = []
        for hh in range(heads):
            in_head = (lane >= hh * head_dim) & (lane < (hh + 1) * head_dim)
            lo = 3 * hh
            hi = 3 * heads + 3 * hh
            own = ((lane >= lo) & (lane < lo + 3)) | ((lane >= hi) & (lane < hi + 3))
            out.append(jnp.concatenate([jnp.where(in_head, q, jnp.zeros_like(q)),
                                        jnp.where(own, extra, jnp.zeros_like(extra))], axis=1))
        return out

    def init_state():
        return (jnp.full((1, tq), NEG_BIG, F32), jnp.zeros((1, tq), F32), jnp.zeros((LANE, tq), F32))

    def finish(q0, states):
        out_t = jnp.zeros((LANE, tq), F32)
        for hh, (_, l_i, acc) in enumerate(states):
            in_head = (sub >= hh * head_dim) & (sub < (hh + 1) * head_dim)
            out_t = jnp.where(in_head, acc / l_i, out_t)
        o_ref[0, pl.ds(q0, tq), :] = out_t.T.astype(o_ref.dtype)

    def keys(k0):
        return kaug[pl.ds(k0, tq), :]

    def values(k0):
        return vt_s[:, pl.ds(k0, tq)]

    n_chain = 2 * heads
    unmasked = [None] * n_chain

    def pair_queries(a):
        return query_tile(2 * a * tq) + query_tile((2 * a + 1) * tq)

    def put_scores(slot, k0, queries):
        for c, s in enumerate(_qk_scores(keys(k0), queries)):
            s_buf[slot, c] = s

    def get_scores(slot):
        return [s_buf[slot, c] for c in range(n_chain)]

    n_pairs = t // (2 * tq)
    queries = pair_queries(0)
    put_scores(0, 0, queries)
    for a in range(n_pairs):
        cur, oth = a % 2, 1 - a % 2

        def kv_tiles(i, states, queries=queries, cur=cur, oth=oth):
            k0 = pl.multiple_of(2 * i * tq, 2 * tq)
            k1 = pl.multiple_of(k0 + tq, tq)
            put_scores(oth, k1, queries)
            states = _softmax_pv(get_scores(cur), [values(k0)] * n_chain, states, unmasked)
            put_scores(cur, pl.multiple_of(k1 + tq, tq), queries)
            return _softmax_pv(get_scores(oth), [values(k1)] * n_chain, states, unmasked)

        states = lax.fori_loop(0, a, kv_tiles, [init_state() for _ in range(n_chain)])
        q_lo, q_hi = 2 * a * tq, (2 * a + 1) * tq
        last = _qk_scores(keys(q_hi), queries[heads:])
        if a + 1 < n_pairs:
            queries = pair_queries(a + 1)
            put_scores(oth, 0, queries)
        states = _softmax_pv(
            get_scores(cur) + last,
            [values(q_lo)] * n_chain + [values(q_hi)] * heads,
            states + [init_state() for _ in range(heads)],
            [causal] * heads + [None] * heads + [causal] * heads)
        finish(q_lo, states[:heads])
        finish(q_hi, [_merge_softmax(states[heads + hh], states[n_chain + hh]) for hh in range(heads)])


def _attn_prompt(qs, k, v, f_row, head_dim):
    b, t, w = qs.shape
    h = f_row.shape[1]
    heads = LANE // head_dim
    assert LANE % head_dim == 0 and w % LANE == 0 and t % (2 * ATT_TILE) == 0 and h % heads == 0
    tok = pl.BlockSpec((1, t, LANE), lambda i, j: (i, 0, j))
    return pl.pallas_call(
        functools.partial(_attn_prompt_kernel, head_dim=head_dim),
        grid=(b, w // LANE),
        in_specs=[tok, tok, tok, pl.BlockSpec((1, 1, heads, t), lambda i, j: (i, j, 0, 0))],
        out_specs=tok,
        out_shape=jax.ShapeDtypeStruct((b, t, w), BF16),
        scratch_shapes=[pltpu.VMEM((t, 2 * LANE), BF16), pltpu.VMEM((t, LANE), BF16),
                        pltpu.VMEM((LANE, t), BF16),
                        pltpu.VMEM((2, 2 * heads, ATT_TILE, ATT_TILE), F32)],
        compiler_params=_params("parallel", "arbitrary"),
        name="attn_prompt",
    )(qs, k, v, f_row.reshape(b, h // heads, heads, t))


def _attn_sample_kernel(q_ref, kn_ref, vn_ref, ck_ref, cv_ref, fn_ref, fnt_ref, fct_ref, o_ref,
                        kall, vall, *, head_dim):
    ts, past = q_ref.shape[1], ck_ref.shape[1]
    heads_per_block = LANE // head_dim
    pair = pl.program_id(1)
    kall[0:past, :] = ck_ref[0].astype(BF16)
    vall[0:past, :] = cv_ref[0].astype(BF16)
    kall[past:, :] = jnp.zeros((LANE, LANE), BF16)
    vall[past:, :] = jnp.zeros((LANE, LANE), BF16)
    kall[past:past + ts, :] = kn_ref[0].astype(BF16)
    vall[past:past + ts, :] = vn_ref[0].astype(BF16)
    lane = lax.broadcasted_iota(jnp.int32, (1, LANE), 1)
    row = lax.broadcasted_iota(jnp.int32, (ts, past + LANE), 0)
    col = lax.broadcasted_iota(jnp.int32, (ts, past + LANE), 1)
    visible = col <= past + row
    q = q_ref[0]
    out = jnp.zeros((ts, LANE), F32)
    for hh in range(heads_per_block):
        head = pair * heads_per_block + hh
        in_head = (lane >= hh * head_dim) & (lane < (hh + 1) * head_dim)
        qm = jnp.where(in_head, q, jnp.zeros_like(q))
        f_cache = fct_ref[0, pl.ds(head, 1), :]
        total = f_cache[:, past - 1:past]
        f_keys = jnp.concatenate([f_cache, total + fnt_ref[0, pl.ds(head, 1), :]], axis=1)
        bias = ((total + _head_column(fn_ref[0], head)) - f_keys) * LOG2_E
        init = (jnp.full((ts, 1), NEG_BIG, F32), jnp.zeros((ts, 1), F32), jnp.zeros((ts, LANE), F32))
        _, l_i, acc = _softmax_step(qm, kall[...], vall[...], bias, init, mask=visible)
        out = jnp.where(in_head, acc / l_i, out)
    o_ref[0] = out.astype(o_ref.dtype)


def _attn_sample(qs, k_new, v_new, cache_k, cache_v, fn_col, fn_row, fc_row, head_dim):
    b, ts, w = qs.shape
    past = cache_k.shape[1]
    h = fn_col.shape[-1]
    assert ts <= LANE and past % LANE == 0
    new = pl.BlockSpec((1, ts, LANE), lambda i, j: (i, 0, j))
    old = pl.BlockSpec((1, past, LANE), lambda i, j: (i, 0, j))
    return pl.pallas_call(
        functools.partial(_attn_sample_kernel, head_dim=head_dim),
        grid=(b, w // LANE),
        in_specs=[new, new, new, old, old,
                  pl.BlockSpec((1, ts, h), lambda i, j: (i, 0, 0)),
                  pl.BlockSpec((1, h, LANE), lambda i, j: (i, 0, 0)),
                  pl.BlockSpec((1, h, past), lambda i, j: (i, 0, 0))],
        out_specs=new,
        out_shape=jax.ShapeDtypeStruct((b, ts, w), BF16),
        scratch_shapes=[pltpu.VMEM((past + LANE, LANE), BF16), pltpu.VMEM((past + LANE, LANE), BF16)],
        compiler_params=_params("parallel", "arbitrary"),
        name="attn_sample",
    )(qs, k_new, v_new, cache_k, cache_v, fn_col, fn_row, fc_row)


def _gate_weights(wa, wx):
    nb, bw, _ = wa.shape
    per = LANE // bw

    def chunked(w):
        w = w.reshape(nb // per, per, bw, bw)
        eye = jnp.eye(per, dtype=w.dtype)
        return jnp.einsum("cpij,pq->cpiqj", w, eye).reshape(nb // per, LANE, LANE)

    return jnp.concatenate([chunked(wa), chunked(wx)], axis=-1).astype(BF16)


def _pad_lanes(x, n):
    return jnp.pad(x, [(0, 0)] * (x.ndim - 1) + [(0, n - x.shape[-1])])


def _mixer(x, mod, w, conv0, h0, attend):
    b, t, d = x.shape
    xr, gr, qs, k, v, logf, ga, gb = _inproj(
        x, mod, w["mix_norm"], w["w_main"], w["w_fgt"], w["w_gate"], w["b_fgt"],
        w["n_heads"], w["q_scale"])
    a_out, conv_state, h_last = _rglru(xr, gr, conv0, h0, w["conv_w"], w["conv_b"], w["w_gates"],
                                       w["rg_ba"], w["rg_bx"], w["rg_lam"])
    att = attend(qs, k, v, logf)
    merge = (a_out, att, ga, gb, w["w_br_rnn"], w["w_br_att"], w["w_out"])
    return merge, k, v, logf, conv_state, h_last.reshape(b, -1)


def kernel(x_prompt, x_sample, cache_k, cache_v, cache_logf, state_conv, state_rglru, c_prompt, c_sample, w_ada, b_ada, ffn1_norm, ffn1_w1, ffn1_w3, ffn1_w2, mix_norm, w_in, b_fgt, conv_w, conv_b, rg_wa, rg_ba, rg_wx, rg_bx, rg_lam, w_br_rnn, w_br_att, w_out, ffn2_norm, ffn2_w1, ffn2_w3, ffn2_w2, final_norm, w_ada_f, b_ada_f):
    bp, seq, d = x_prompt.shape
    bs, dec_seq, _ = x_sample.shape
    depth = w_ada.shape[0]
    n_heads = b_fgt.shape[-1]
    head_dim = cache_k.shape[-1]
    d_rnn = conv_w.shape[-1]
    kw = conv_w.shape[1]
    att_w = n_heads * head_dim
    past = cache_k.shape[2]
    assert d_rnn == d and att_w == d and n_heads <= LANE
    assert w_in.shape[-1] == 2 * d_rnn + 3 * att_w + n_heads + 2 * d

    c_all = jnp.concatenate([c_prompt, c_sample], axis=0)
    mod_f = _ada_mod(c_all, w_ada_f, b_ada_f).reshape(bp + bs, 2, d)
    xp, xs = x_prompt, x_sample
    outs_p, outs_s = [], []
    for l in range(depth):
        mod = _ada_mod(c_all, w_ada[l], b_ada[l]).reshape(bp + bs, N_MOD, d)
        mp, ms = mod[:bp], mod[bp:]
        n_main = 2 * d_rnn + 3 * att_w
        w = {
            "mix_norm": mix_norm[l], "n_heads": n_heads, "q_scale": head_dim ** -0.5 * LOG2_E,
            "w_main": w_in[l, :, :n_main].astype(BF16),
            "w_fgt": _pad_lanes(w_in[l, :, n_main:n_main + n_heads], LANE).astype(BF16),
            "w_gate": w_in[l, :, n_main + n_heads:].astype(BF16),
            "b_fgt": _pad_lanes(b_fgt[l].reshape(1, n_heads), LANE),
            "conv_w": conv_w[l], "conv_b": conv_b[l], "w_gates": _gate_weights(rg_wa[l], rg_wx[l]),
            "rg_ba": rg_ba[l], "rg_bx": rg_bx[l], "rg_lam": rg_lam[l],
            "w_br_rnn": w_br_rnn[l].astype(BF16), "w_br_att": w_br_att[l].astype(BF16),
            "w_out": w_out[l].astype(BF16),
        }
        f1 = (ffn1_norm[l], ffn1_w1[l].astype(BF16), ffn1_w3[l].astype(BF16), ffn1_w2[l].astype(BF16))
        f2 = (ffn2_norm[l], ffn2_w1[l].astype(BF16), ffn2_w3[l].astype(BF16), ffn2_w2[l].astype(BF16))
        last = l == depth - 1

        xp = _ffn(xp, mp, *f1, base=0)
        xs = _ffn(xs, ms, *f1, base=0)

        def attend_prompt(qs, k, v, logf):
            return _attn_prompt(qs, k, v, _cumsum_lanes(jnp.swapaxes(logf, 1, 2)), head_dim)

        def attend_sample(qs, k, v, logf):
            fc_row = _cumsum_lanes(jnp.swapaxes(cache_logf[l], 1, 2))
            fn_row = _cumsum_lanes(_pad_lanes(jnp.swapaxes(logf, 1, 2), LANE))
            fn_col = jnp.swapaxes(fn_row[:, :, :dec_seq], 1, 2)
            return _attn_sample(qs, k, v, cache_k[l].reshape(bs, past, att_w),
                                cache_v[l].reshape(bs, past, att_w), fn_col, fn_row, fc_row, head_dim)

        merge_p, kp, vp, lp, cp, hp = _mixer(xp, mp, w, jnp.zeros((bp, kw - 1, d_rnn), F32),
                                             jnp.zeros((bp, d_rnn), F32), attend_prompt)
        merge_s, ks, vs, ls, cs, hs = _mixer(xs, ms, w, state_conv[l], state_rglru[l], attend_sample)

        xp = _ffn(xp, mp, *f2, base=6, merge=merge_p,
                  final=(final_norm, mod_f[:bp]) if last else None)
        xs = _ffn(xs, ms, *f2, base=6, merge=merge_s,
                  final=(final_norm, mod_f[bp:]) if last else None)
        heads_p = (bp, seq, n_heads, head_dim)
        heads_s = (bs, dec_seq, n_heads, head_dim)
        outs_p.append((kp.reshape(heads_p), vp.reshape(heads_p), lp, cp, hp))
        outs_s.append((ks.reshape(heads_s), vs.reshape(heads_s), ls, cs, hs))
    stack = lambda outs: tuple(jnp.stack(leaf) for leaf in zip(*outs))
    return (xp, xs) + stack(outs_p) + stack(outs_s)
```

```python
import functools
import math

import jax
import jax.numpy as jnp
from jax import lax
from jax.experimental import pallas as pl
from jax.experimental.pallas import tpu as pltpu

F32 = jnp.float32
BF16 = jnp.bfloat16

LANE = 128
SUBLANE = 8
VMEM_BYTES_V7X = 64 * 1024 * 1024
VMEM_LIMIT = VMEM_BYTES_V7X - 8 * 1024 * 1024

EPS = 1e-6
MACARON_W = 0.5
RG_C = 8.0
N_MOD = 9

ROWS_PER_STEP = 512
SCAN_ROWS = 256
ATT_TILE = 256
CUMSUM_CHUNK = 128
CUMSUM_ROWS = 128
NEG_BIG = -1e30
LOG2_E = math.log2(math.e)


def _params(*sem):
    return pltpu.CompilerParams(dimension_semantics=sem, vmem_limit_bytes=VMEM_LIMIT)


def _resident(shape):
    nd = len(shape)
    return pl.BlockSpec(shape, lambda *_: (0,) * nd, pipeline_mode=pl.Buffered(1))


def _row_blocks(b, t, rows):
    if t >= rows:
        assert t % rows == 0, (t, rows)
        return 1, rows
    bb = max(1, min(b, rows // t))
    while b % bb:
        bb -= 1
    return bb, t


def _mod_norm(x, g, shift, scale):
    inv = lax.rsqrt(jnp.mean(x * x, axis=-1, keepdims=True) + EPS)
    return (x * inv) * g * (1.0 + scale) + shift


def _dot(a, b):
    return jnp.dot(a, b, preferred_element_type=F32)


def _log_sigmoid(x):
    return jnp.minimum(x, 0.0) - jnp.log1p(jnp.exp(-jnp.abs(x)))


def _ada_kernel(c_ref, w_ref, b_ref, o_ref):
    c = c_ref[...]
    s = (c * jax.nn.sigmoid(c)).astype(BF16)
    o_ref[...] = _dot(s, w_ref[...].astype(BF16)) + b_ref[...]


def _ada_mod(c, w, b):
    bc, d = c.shape
    n = w.shape[1]
    tn = 1024
    assert n % tn == 0
    return pl.pallas_call(
        _ada_kernel,
        grid=(n // tn,),
        in_specs=[
            pl.BlockSpec((bc, d), lambda j: (0, 0)),
            pl.BlockSpec((d, tn), lambda j: (0, j)),
            pl.BlockSpec((1, tn), lambda j: (0, j)),
        ],
        out_specs=pl.BlockSpec((bc, tn), lambda j: (0, j)),
        out_shape=jax.ShapeDtypeStruct((bc, n), F32),
        compiler_params=_params("parallel"),
        name="ada_mod",
    )(c, w, b.reshape(1, n))


def _ffn_kernel(x_ref, mod_ref, *rest, base, merge, final):
    rest = list(rest)
    x = x_ref[...]
    bb, tm, d = x.shape
    if merge:
        a_ref, t_ref, ga_ref, gb_ref, wr_ref, wa_ref, wo_ref = rest[:7]
        rest = rest[7:]
        rows = bb * tm
        pr = _dot(a_ref[...].reshape(rows, -1), wr_ref[...])
        pa = _dot(t_ref[...].reshape(rows, -1), wa_ref[...])
        z = (jax.nn.sigmoid(ga_ref[...].reshape(rows, d)) * pr
             + jax.nn.sigmoid(gb_ref[...].reshape(rows, d)) * pa)
        x = x + (1.0 + mod_ref[:, 5:6, :]) * _dot(z.astype(BF16), wo_ref[...]).reshape(bb, tm, d)
    g_ref, w1_ref, w3_ref, w2_ref = rest[:4]
    rest = rest[4:]
    if final:
        fg_ref, fmod_ref, o_ref = rest
    else:
        (o_ref,) = rest
    shift = mod_ref[:, base:base + 1, :]
    scale = mod_ref[:, base + 1:base + 2, :]
    gate = 1.0 + mod_ref[:, base + 2:base + 3, :]
    hb = _mod_norm(x, g_ref[...], shift, scale).reshape(bb * tm, d).astype(BF16)
    a = _dot(hb, w1_ref[...])
    b = _dot(hb, w3_ref[...])
    t = ((a * jax.nn.sigmoid(a)) * b).astype(BF16)
    y = _dot(t, w2_ref[...]).reshape(bb, tm, d)
    out = x + MACARON_W * gate * y
    if final:
        out = _mod_norm(out, fg_ref[...], fmod_ref[:, 0:1, :], fmod_ref[:, 1:2, :])
    o_ref[...] = out


def _ffn(x, mod, g, w1, w3, w2, base, merge=None, final=None):
    b, t, d = x.shape
    dff = w1.shape[1]
    bb, tm = _row_blocks(b, t, ROWS_PER_STEP)
    tok = lambda n: pl.BlockSpec((bb, tm, n), lambda i, j: (i, j, 0))
    modspec = lambda n: pl.BlockSpec((bb, n, d), lambda i, j: (i, 0, 0))
    in_specs = [tok(d), modspec(mod.shape[1])]
    args = [x, mod]
    if merge is not None:
        a_out, att, ga, gb, w_br_rnn, w_br_att, w_out = merge
        in_specs += [tok(a_out.shape[-1]), tok(att.shape[-1]), tok(d), tok(d),
                     _resident(w_br_rnn.shape), _resident(w_br_att.shape), _resident(w_out.shape)]
        args += list(merge)
    in_specs += [_resident((1, d)), _resident((d, dff)), _resident((d, dff)), _resident((dff, d))]
    args += [g.reshape(1, d), w1, w3, w2]
    if final is not None:
        fg, fmod = final
        in_specs += [_resident((1, d)), modspec(fmod.shape[1])]
        args += [fg.reshape(1, d), fmod]
    return pl.pallas_call(
        functools.partial(_ffn_kernel, base=base, merge=merge is not None, final=final is not None),
        grid=(b // bb, t // tm),
        in_specs=in_specs,
        out_specs=tok(d),
        out_shape=jax.ShapeDtypeStruct((b, t, d), F32),
        compiler_params=_params("parallel", "parallel"),
        name="ffn" + ("_merge" if merge is not None else "") + ("_final" if final is not None else ""),
    )(*args)


def _inproj_kernel(x_ref, mod_ref, g_ref, wm_ref, wf_ref, wg_ref, bf_ref,
                   xr_o, gr_o, q_o, k_o, v_o, lf_o, ga_o, gb_o, *, q_scale):
    x = x_ref[...]
    bb, tm, d = x.shape
    hb = _mod_norm(x, g_ref[...], mod_ref[:, 3:4, :], mod_ref[:, 4:5, :])
    hb = hb.reshape(bb * tm, d).astype(BF16)

    def proj(w_ref, idx):
        return _dot(hb, w_ref[:, idx * d:(idx + 1) * d]).reshape(bb, tm, d)

    xr_o[...] = proj(wm_ref, 0)
    gr_o[...] = proj(wm_ref, 1)
    q_o[...] = (proj(wm_ref, 2) * q_scale).astype(q_o.dtype)
    k_o[...] = proj(wm_ref, 3)
    v_o[...] = proj(wm_ref, 4)
    ga_o[...] = proj(wg_ref, 0)
    gb_o[...] = proj(wg_ref, 1)
    n_heads = lf_o.shape[-1]
    lf = _log_sigmoid(_dot(hb, wf_ref[...]) + bf_ref[...])
    lf_o[...] = lf[:, :n_heads].reshape(bb, tm, n_heads)


def _inproj(x, mod, g, w_main, w_fgt, w_gate, b_fgt_pad, n_heads, q_scale):
    b, t, d = x.shape
    bb, tm = _row_blocks(b, t, ROWS_PER_STEP)
    tok = pl.BlockSpec((bb, tm, d), lambda i, j: (i, j, 0))
    f32_out = jax.ShapeDtypeStruct((b, t, d), F32)
    return pl.pallas_call(
        functools.partial(_inproj_kernel, q_scale=q_scale),
        grid=(b // bb, t // tm),
        in_specs=[tok, pl.BlockSpec((bb, mod.shape[1], d), lambda i, j: (i, 0, 0)),
                  _resident((1, d)), _resident(w_main.shape), _resident(w_fgt.shape),
                  _resident(w_gate.shape), _resident((1, LANE))],
        out_specs=[tok, tok, tok, tok, tok,
                   pl.BlockSpec((bb, tm, n_heads), lambda i, j: (i, j, 0)), tok, tok],
        out_shape=[f32_out, f32_out, jax.ShapeDtypeStruct((b, t, d), BF16), f32_out, f32_out,
                   jax.ShapeDtypeStruct((b, t, n_heads), F32), f32_out, f32_out],
        compiler_params=_params("parallel", "parallel"),
        name="in_proj",
    )(x, mod, g.reshape(1, d), w_main, w_fgt, w_gate, b_fgt_pad)


def _linear_scan(a, u, h_in):
    t = a.shape[0]
    n = t // SUBLANE
    a3 = a.reshape(n, SUBLANE, LANE)
    u3 = u.reshape(n, SUBLANE, LANE)
    row = lax.broadcasted_iota(jnp.int32, (n, SUBLANE, LANE), 1)
    for dist in (1, 2, 4):
        keep = row >= dist
        u_prev = pltpu.roll(u3, dist, axis=1)
        a_prev = pltpu.roll(a3, dist, axis=1)
        u3 = jnp.where(keep, a3 * u_prev, 0.0) + u3
        a3 = jnp.where(keep, a3 * a_prev, a3)
    h = h_in
    outs = []
    for grp in range(n):
        hg = u3[grp] + a3[grp] * h
        outs.append(hg)
        h = hg[SUBLANE - 1:SUBLANE, :]
    return jnp.concatenate(outs, axis=0)


def _lru_coefficients(xc, wg_ref, ba_ref, bx_ref, lam_ref):
    a_parts, u_parts = [], []
    for ch in range(xc.shape[1] // LANE):
        sl = slice(ch * LANE, (ch + 1) * LANE)
        xcc = xc[:, sl]
        gates = _dot(xcc.astype(BF16), wg_ref[ch])
        r = jax.nn.sigmoid(gates[:, :LANE] + ba_ref[:, sl])
        i = jax.nn.sigmoid(gates[:, LANE:] + bx_ref[:, sl])
        neg_lam = -lam_ref[:, sl]
        softplus = jnp.maximum(neg_lam, 0.0) + jnp.log1p(jnp.exp(-jnp.abs(neg_lam)))
        log_a = (-RG_C) * r * softplus
        a = jnp.exp(log_a)
        a_parts.append(a)
        u_parts.append(jnp.sqrt(-jnp.tanh(log_a) * (a * a + 1.0)) * (i * xcc))
    return a_parts, u_parts


def _rglru_rows(xr_ref, gr_ref, cw_ref, cb_ref, coeff_refs, aout_ref, buf, hcar):
    tc = xr_ref.shape[1]
    kw = cw_ref.shape[0]
    lo = SUBLANE - (kw - 1)
    buf[SUBLANE:SUBLANE + tc, :] = xr_ref[0]
    xc = cb_ref[...]
    for j in range(kw):
        xc = xc + buf[lo + j:lo + j + tc, :] * cw_ref[j:j + 1, :]
    buf[lo:SUBLANE, :] = buf[SUBLANE + tc - (kw - 1):SUBLANE + tc, :]
    a_parts, u_parts = _lru_coefficients(xc, *coeff_refs)
    for ch, (a, u) in enumerate(zip(a_parts, u_parts)):
        sl = slice(ch * LANE, (ch + 1) * LANE)
        hs = _linear_scan(a, u, hcar[:, sl])
        hcar[:, sl] = hs[tc - 1:tc, :]
        aout_ref[0, :, sl] = (jax.nn.gelu(gr_ref[0, :, sl]) * hs).astype(aout_ref.dtype)


def _rglru_segments(xr_ref, gr_ref, cw_ref, cb_ref, coeff_refs, aout_ref, buf, hcar, xs, hs_s):
    tc, c = xr_ref.shape[1], xr_ref.shape[2]
    kw = cw_ref.shape[0]
    seg = tc // SUBLANE
    n_ch = c // LANE
    sub = lax.broadcasted_iota(jnp.int32, (SUBLANE, c), 0)

    for s in range(SUBLANE):
        for ch in range(n_ch):
            xs[ch, pl.ds(s, seg, stride=SUBLANE), :] = xr_ref[0, s * seg:(s + 1) * seg,
                                                             ch * LANE:(ch + 1) * LANE]
    x = jnp.concatenate([xs[ch] for ch in range(n_ch)], axis=1)

    wrapped = []
    for i in range(kw - 1):
        p = seg - (kw - 1) + i
        cur = x[p * SUBLANE:(p + 1) * SUBLANE, :]
        prev = buf[i * SUBLANE:(i + 1) * SUBLANE, :]
        wrapped.append(pltpu.roll(jnp.where(sub == SUBLANE - 1, prev, cur), 1, axis=0))
    buf[...] = x[(seg - (kw - 1)) * SUBLANE:, :]
    xc = cb_ref[...] + x * cw_ref[kw - 1:kw, :]
    for j in range(1, kw):
        shifted = jnp.concatenate(wrapped[kw - 1 - j:] + [x[:(seg - j) * SUBLANE, :]], axis=0)
        xc = xc + shifted * cw_ref[kw - 1 - j:kw - j, :]

    a_parts, u_parts = _lru_coefficients(xc, *coeff_refs)
    a = jnp.concatenate(a_parts, axis=1)
    u = jnp.concatenate(u_parts, axis=1)

    h = u[:SUBLANE, :]
    prod = a[:SUBLANE, :]
    local, prods = [h], [prod]
    for p in range(1, seg):
        a_p = a[p * SUBLANE:(p + 1) * SUBLANE, :]
        h = a_p * h + u[p * SUBLANE:(p + 1) * SUBLANE, :]
        prod = a_p * prod
        local.append(h)
        prods.append(prod)
    state = hcar[...]
    entering = []
    for s in range(SUBLANE):
        entering.append(state)
        state = h[s:s + 1, :] + prod[s:s + 1, :] * state
    hcar[...] = state
    h_in = jnp.concatenate(entering, axis=0)
    for p in range(seg):
        full = local[p] + prods[p] * h_in
        for ch in range(n_ch):
            hs_s[ch, p * SUBLANE:(p + 1) * SUBLANE, :] = full[:, ch * LANE:(ch + 1) * LANE]

    for ch in range(n_ch):
        sl = slice(ch * LANE, (ch + 1) * LANE)
        hn = jnp.concatenate([hs_s[ch, pl.ds(s, seg, stride=SUBLANE), :] for s in range(SUBLANE)], axis=0)
        aout_ref[0, :, sl] = (jax.nn.gelu(gr_ref[0, :, sl]) * hn).astype(aout_ref.dtype)


def _rglru_kernel(xr_ref, gr_ref, conv0_ref, h0_ref, cw_ref, cb_ref, wg_ref, ba_ref, bx_ref, lam_ref,
                  aout_ref, convo_ref, hlast_ref, buf, hcar, *stage):
    tc = xr_ref.shape[1]
    kw = cw_ref.shape[0]
    segmented = bool(stage)

    @pl.when(pl.program_id(1) == 0)
    def _():
        hcar[...] = h0_ref[0]
        if segmented:
            buf[...] = jnp.zeros(buf.shape, F32)
            for i in range(kw - 1):
                buf[i * SUBLANE + SUBLANE - 1:(i + 1) * SUBLANE, :] = conv0_ref[0, i:i + 1, :]
        else:
            buf[SUBLANE - (kw - 1):SUBLANE, :] = conv0_ref[0]

    coeff_refs = (wg_ref, ba_ref, bx_ref, lam_ref)
    if segmented:
        _rglru_segments(xr_ref, gr_ref, cw_ref, cb_ref, coeff_refs, aout_ref, buf, hcar, *stage)
    else:
        _rglru_rows(xr_ref, gr_ref, cw_ref, cb_ref, coeff_refs, aout_ref, buf, hcar)
    convo_ref[0] = xr_ref[0, tc - (kw - 1):tc, :]
    hlast_ref[0] = hcar[...]


def _rglru(xr, gr, conv0, h0, conv_w, conv_b, w_gates, ba, bx, lam):
    b, t, c = xr.shape
    kw = conv_w.shape[0]
    tc = min(t, SCAN_ROWS)
    assert t % tc == 0 and tc % SUBLANE == 0 and tc >= kw - 1 and kw - 1 <= SUBLANE
    tok = pl.BlockSpec((1, tc, c), lambda i, j: (i, j, 0))
    per_b = lambda n: pl.BlockSpec((1, n, c), lambda i, j: (i, 0, 0))
    vec = _resident((1, c))
    if tc // SUBLANE >= SUBLANE:
        staging = pltpu.VMEM((c // LANE, tc, LANE), F32)
        scratch = [pltpu.VMEM(((kw - 1) * SUBLANE, c), F32), pltpu.VMEM((1, c), F32), staging, staging]
    else:
        scratch = [pltpu.VMEM((SUBLANE + tc, c), F32), pltpu.VMEM((1, c), F32)]
    return pl.pallas_call(
        _rglru_kernel,
        grid=(b, t // tc),
        in_specs=[tok, tok, per_b(kw - 1), per_b(1), _resident((kw, c)), vec,
                  _resident(w_gates.shape), vec, vec, vec],
        out_specs=[tok, per_b(kw - 1), per_b(1)],
        out_shape=[jax.ShapeDtypeStruct((b, t, c), BF16),
                   jax.ShapeDtypeStruct((b, kw - 1, c), F32),
                   jax.ShapeDtypeStruct((b, 1, c), F32)],
        scratch_shapes=scratch,
        compiler_params=_params("parallel", "arbitrary"),
        name="rglru",
    )(xr, gr, conv0, h0.reshape(b, 1, c), conv_w, conv_b.reshape(1, c), w_gates,
      ba.reshape(1, c), bx.reshape(1, c), lam.reshape(1, c))


def _cumsum_kernel(x_ref, o_ref):
    rows, t = x_ref.shape
    r = lax.broadcasted_iota(jnp.int32, (CUMSUM_CHUNK, CUMSUM_CHUNK), 0)
    c = lax.broadcasted_iota(jnp.int32, (CUMSUM_CHUNK, CUMSUM_CHUNK), 1)
    upper = (r <= c).astype(F32)
    carry = jnp.zeros((rows, 1), F32)
    for i in range(t // CUMSUM_CHUNK):
        sl = slice(i * CUMSUM_CHUNK, (i + 1) * CUMSUM_CHUNK)
        cs = jnp.dot(x_ref[:, sl], upper, preferred_element_type=F32,
                     precision=lax.Precision.HIGHEST) + carry
        o_ref[:, sl] = cs
        carry = cs[:, CUMSUM_CHUNK - 1:CUMSUM_CHUNK]


def _cumsum_lanes(x):
    b, h, t = x.shape
    n = b * h
    rows = min(n, CUMSUM_ROWS)
    assert t % CUMSUM_CHUNK == 0 and n % rows == 0
    spec = pl.BlockSpec((rows, t), lambda i: (i, 0))
    return pl.pallas_call(
        _cumsum_kernel, grid=(n // rows,), in_specs=[spec], out_specs=spec,
        out_shape=jax.ShapeDtypeStruct((n, t), F32),
        compiler_params=_params("parallel"), name="fgt_cumsum",
    )(x.reshape(n, t)).reshape(b, h, t)


def _head_column(f_blk, head):
    lane = lax.broadcasted_iota(jnp.int32, f_blk.shape, 1)
    return jnp.sum(jnp.where(lane == head, f_blk, 0.0), axis=-1, keepdims=True)


def _softmax_step(qm, kt, vt, bias, carry, mask=None):
    m_i, l_i, acc = carry
    s = lax.dot_general(qm, kt, (((1,), (1,)), ((), ())), preferred_element_type=F32) + bias
    if mask is not None:
        s = jnp.where(mask, s, -jnp.inf)
    m_new = jnp.maximum(m_i, jnp.max(s, axis=-1, keepdims=True))
    alpha = jnp.exp2(m_i - m_new)
    p = jnp.exp2(s - m_new)
    l_new = alpha * l_i + jnp.sum(p, axis=-1, keepdims=True)
    acc = alpha * acc + _dot(p.astype(BF16), vt)
    return m_new, l_new, acc


def _split3(x):
    hi = x.astype(BF16).astype(F32)
    rest = x - hi
    mid = rest.astype(BF16).astype(F32)
    lo = (rest - mid).astype(BF16).astype(F32)
    return hi, mid, lo


def _bias_lanes(f_rows):
    heads, t = f_rows.shape
    n = 3 * heads
    terms = _split3(f_rows)
    rows = 2 * SUBLANE
    assert 2 * n <= rows
    row = lax.broadcasted_iota(jnp.int32, (rows, t), 0)
    packed = jnp.zeros((rows, t), F32)
    for hh in range(heads):
        for j in range(3):
            term = terms[j][hh:hh + 1, :]
            packed = jnp.where((row == 3 * hh + j) | (row == n + 3 * hh + j), term, packed)
    cols = jnp.concatenate([packed, jnp.zeros((LANE - rows, t), F32)], axis=0).T
    lane = lax.broadcasted_iota(jnp.int32, (1, LANE), 1)
    key_lanes = jnp.where(lane < n, -cols, jnp.where(lane < 2 * n, 1.0, 0.0))
    query_lanes = jnp.where(lane < n, 1.0, jnp.where(lane < 2 * n, cols, 0.0))
    return key_lanes, query_lanes


def _qk_scores(kt, queries):
    return [lax.dot_general(kt, qa, (((1,), (1,)), ((), ())), preferred_element_type=F32)
            for qa in queries]


def _softmax_pv(scores, values, states, masks):
    partial = []
    for s, (m_i, l_i, acc), mask in zip(scores, states, masks):
        if mask is not None:
            s = jnp.where(mask, s, -jnp.inf)
        m_new = jnp.maximum(m_i, jnp.max(s, axis=0, keepdims=True))
        alpha = jnp.exp2(m_i - m_new)
        p = jnp.exp2(s - m_new)
        l_new = alpha * l_i + jnp.sum(p, axis=0, keepdims=True)
        partial.append((m_new, l_new, alpha, p.astype(BF16)))
    return [(m_new, l_new, alpha * acc + _dot(vt, p))
            for (m_new, l_new, alpha, p), (_, _, acc), vt in zip(partial, states, values)]


def _merge_softmax(a, b):
    (m_a, l_a, acc_a), (m_b, l_b, acc_b) = a, b
    m = jnp.maximum(m_a, m_b)
    w_a = jnp.exp2(m_a - m)
    w_b = jnp.exp2(m_b - m)
    return m, w_a * l_a + w_b * l_b, w_a * acc_a + w_b * acc_b


def _attn_prompt_kernel(q_ref, k_ref, v_ref, f_ref, o_ref, kaug, qext, vt_s, s_buf, *, head_dim):
    t = q_ref.shape[1]
    tq = ATT_TILE
    heads = LANE // head_dim
    lane = lax.broadcasted_iota(jnp.int32, (1, LANE), 1)
    sub = lax.broadcasted_iota(jnp.int32, (LANE, 1), 0)
    key_pos = lax.broadcasted_iota(jnp.int32, (tq, tq), 0)
    qry_pos = lax.broadcasted_iota(jnp.int32, (tq, tq), 1)
    causal = key_pos <= qry_pos

    key_lanes, query_lanes = _bias_lanes(f_ref[0, 0] * LOG2_E)
    kaug[:, :LANE] = k_ref[0].astype(BF16)
    kaug[:, LANE:] = key_lanes.astype(BF16)
    qext[...] = query_lanes.astype(BF16)
    vt_s[...] = v_ref[0].T.astype(BF16)

    def query_tile(q0):
        q = q_ref[0, pl.ds(q0, tq), :]
        extra = qext[pl.ds(q0, tq), :]
        out = []
        for hh in range(heads):
            in_head = (lane >= hh * head_dim) & (lane < (hh + 1) * head_dim)
            lo = 3 * hh
            hi = 3 * heads + 3 * hh
            own = ((lane >= lo) & (lane < lo + 3)) | ((lane >= hi) & (lane < hi + 3))
            out.append(jnp.concatenate([jnp.where(in_head, q, jnp.zeros_like(q)),
                                        jnp.where(own, extra, jnp.zeros_like(extra))], axis=1))
        return out

    def init_state():
        return (jnp.full((1, tq), NEG_BIG, F32), jnp.zeros((1, tq), F32), jnp.zeros((LANE, tq), F32))

    def finish(q0, states):
        out_t = jnp.zeros((LANE, tq), F32)
        for hh, (_, l_i, acc) in enumerate(states):
            in_head = (sub >= hh * head_dim) & (sub < (hh + 1) * head_dim)
            out_t = jnp.where(in_head, acc / l_i, out_t)
        o_ref[0, pl.ds(q0, tq), :] = out_t.T.astype(o_ref.dtype)

    def keys(k0):
        return kaug[pl.ds(k0, tq), :]

    def values(k0):
        return vt_s[:, pl.ds(k0, tq)]

    n_chain = 2 * heads
    unmasked = [None] * n_chain

    def pair_queries(a):
        return query_tile(2 * a * tq) + query_tile((2 * a + 1) * tq)

    def put_scores(slot, k0, queries):
        for c, s in enumerate(_qk_scores(keys(k0), queries)):
            s_buf[slot, c] = s

    def get_scores(slot):
        return [s_buf[slot, c] for c in range(n_chain)]

    n_pairs = t // (2 * tq)
    queries = pair_queries(0)
    put_scores(0, 0, queries)
    for a in range(n_pairs):
        cur, oth = a % 2, 1 - a % 2

        states = [init_state() for _ in range(n_chain)]
        for i in range(a):
            k0 = 2 * i * tq
            put_scores(oth, k0 + tq, queries)
            states = _softmax_pv(get_scores(cur), [values(k0)] * n_chain, states, unmasked)
            put_scores(cur, k0 + 2 * tq, queries)
            states = _softmax_pv(get_scores(oth), [values(k0 + tq)] * n_chain, states, unmasked)
        q_lo, q_hi = 2 * a * tq, (2 * a + 1) * tq
        last = _qk_scores(keys(q_hi), queries[heads:])
        if a + 1 < n_pairs:
            queries = pair_queries(a + 1)
            put_scores(oth, 0, queries)
        states = _softmax_pv(
            get_scores(cur) + last,
            [values(q_lo)] * n_chain + [values(q_hi)] * heads,
            states + [init_state() for _ in range(heads)],
            [causal] * heads + [None] * heads + [causal] * heads)
        finish(q_lo, states[:heads])
        finish(q_hi, [_merge_softmax(states[heads + hh], states[n_chain + hh]) for hh in range(heads)])


def _attn_prompt(qs, k, v, f_row, head_dim):
    b, t, w = qs.shape
    h = f_row.shape[1]
    heads = LANE // head_dim
    assert LANE % head_dim == 0 and w % LANE == 0 and t % (2 * ATT_TILE) == 0 and h % heads == 0
    tok = pl.BlockSpec((1, t, LANE), lambda i, j: (i, 0, j))
    return pl.pallas_call(
        functools.partial(_attn_prompt_kernel, head_dim=head_dim),
        grid=(b, w // LANE),
        in_specs=[tok, tok, tok, pl.BlockSpec((1, 1, heads, t), lambda i, j: (i, j, 0, 0))],
        out_specs=tok,
        out_shape=jax.ShapeDtypeStruct((b, t, w), BF16),
        scratch_shapes=[pltpu.VMEM((t, 2 * LANE), BF16), pltpu.VMEM((t, LANE), BF16),
                        pltpu.VMEM((LANE, t), BF16),
                        pltpu.VMEM((2, 2 * heads, ATT_TILE, ATT_TILE), F32)],
        compiler_params=_params("parallel", "arbitrary"),
        name="attn_prompt",
    )(qs, k, v, f_row.reshape(b, h // heads, heads, t))


def _attn_sample_kernel(q_ref, kn_ref, vn_ref, ck_ref, cv_ref, fn_ref, fnt_ref, fct_ref, o_ref,
                        kall, vall, *, head_dim):
    ts, past = q_ref.shape[1], ck_ref.shape[1]
    heads_per_block = LANE // head_dim
    pair = pl.program_id(1)
    kall[0:past, :] = ck_ref[0].astype(BF16)
    vall[0:past, :] = cv_ref[0].astype(BF16)
    kall[past:, :] = jnp.zeros((LANE, LANE), BF16)
    vall[past:, :] = jnp.zeros((LANE, LANE), BF16)
    kall[past:past + ts, :] = kn_ref[0].astype(BF16)
    vall[past:past + ts, :] = vn_ref[0].astype(BF16)
    lane = lax.broadcasted_iota(jnp.int32, (1, LANE), 1)
    row = lax.broadcasted_iota(jnp.int32, (ts, past + LANE), 0)
    col = lax.broadcasted_iota(jnp.int32, (ts, past + LANE), 1)
    visible = col <= past + row
    q = q_ref[0]
    out = jnp.zeros((ts, LANE), F32)
    for hh in range(heads_per_block):
        head = pair * heads_per_block + hh
        in_head = (lane >= hh * head_dim) & (lane < (hh + 1) * head_dim)
        qm = jnp.where(in_head, q, jnp.zeros_like(q))
        f_cache = fct_ref[0, pl.ds(head, 1), :]
        total = f_cache[:, past - 1:past]
        f_keys = jnp.concatenate([f_cache, total + fnt_ref[0, pl.ds(head, 1), :]], axis=1)
        bias = ((total + _head_column(fn_ref[0], head)) - f_keys) * LOG2_E
        init = (jnp.full((ts, 1), NEG_BIG, F32), jnp.zeros((ts, 1), F32), jnp.zeros((ts, LANE), F32))
        _, l_i, acc = _softmax_step(qm, kall[...], vall[...], bias, init, mask=visible)
        out = jnp.where(in_head, acc / l_i, out)
    o_ref[0] = out.astype(o_ref.dtype)


def _attn_sample(qs, k_new, v_new, cache_k, cache_v, fn_col, fn_row, fc_row, head_dim):
    b, ts, w = qs.shape
    past = cache_k.shape[1]
    h = fn_col.shape[-1]
    assert ts <= LANE and past % LANE == 0
    new = pl.BlockSpec((1, ts, LANE), lambda i, j: (i, 0, j))
    old = pl.BlockSpec((1, past, LANE), lambda i, j: (i, 0, j))
    return pl.pallas_call(
        functools.partial(_attn_sample_kernel, head_dim=head_dim),
        grid=(b, w // LANE),
        in_specs=[new, new, new, old, old,
                  pl.BlockSpec((1, ts, h), lambda i, j: (i, 0, 0)),
                  pl.BlockSpec((1, h, LANE), lambda i, j: (i, 0, 0)),
                  pl.BlockSpec((1, h, past), lambda i, j: (i, 0, 0))],
        out_specs=new,
        out_shape=jax.ShapeDtypeStruct((b, ts, w), BF16),
        scratch_shapes=[pltpu.VMEM((past + LANE, LANE), BF16), pltpu.VMEM((past + LANE, LANE), BF16)],
        compiler_params=_params("parallel", "arbitrary"),
        name="attn_sample",
    )(qs, k_new, v_new, cache_k, cache_v, fn_col, fn_row, fc_row)


def _gate_weights(wa, wx):
    nb, bw, _ = wa.shape
    per = LANE // bw

    def chunked(w):
        w = w.reshape(nb // per, per, bw, bw)
        eye = jnp.eye(per, dtype=w.dtype)
        return jnp.einsum("cpij,pq->cpiqj", w, eye).reshape(nb // per, LANE, LANE)

    return jnp.concatenate([chunked(wa), chunked(wx)], axis=-1).astype(BF16)


def _pad_lanes(x, n):
    return jnp.pad(x, [(0, 0)] * (x.ndim - 1) + [(0, n - x.shape[-1])])


def _mixer(x, mod, w, conv0, h0, attend):
    b, t, d = x.shape
    xr, gr, qs, k, v, logf, ga, gb = _inproj(
        x, mod, w["mix_norm"], w["w_main"], w["w_fgt"], w["w_gate"], w["b_fgt"],
        w["n_heads"], w["q_scale"])
    a_out, conv_state, h_last = _rglru(xr, gr, conv0, h0, w["conv_w"], w["conv_b"], w["w_gates"],
                                       w["rg_ba"], w["rg_bx"], w["rg_lam"])
    att = attend(qs, k, v, logf)
    merge = (a_out, att, ga, gb, w["w_br_rnn"], w["w_br_att"], w["w_out"])
    return merge, k, v, logf, conv_state, h_last.reshape(b, -1)


def kernel(x_prompt, x_sample, cache_k, cache_v, cache_logf, state_conv, state_rglru, c_prompt, c_sample, w_ada, b_ada, ffn1_norm, ffn1_w1, ffn1_w3, ffn1_w2, mix_norm, w_in, b_fgt, conv_w, conv_b, rg_wa, rg_ba, rg_wx, rg_bx, rg_lam, w_br_rnn, w_br_att, w_out, ffn2_norm, ffn2_w1, ffn2_w3, ffn2_w2, final_norm, w_ada_f, b_ada_f):
    bp, seq, d = x_prompt.shape
    bs, dec_seq, _ = x_sample.shape
    depth = w_ada.shape[0]
    n_heads = b_fgt.shape[-1]
    head_dim = cache_k.shape[-1]
    d_rnn = conv_w.shape[-1]
    kw = conv_w.shape[1]
    att_w = n_heads * head_dim
    past = cache_k.shape[2]
    assert d_rnn == d and att_w == d and n_heads <= LANE
    assert w_in.shape[-1] == 2 * d_rnn + 3 * att_w + n_heads + 2 * d

    c_all = jnp.concatenate([c_prompt, c_sample], axis=0)
    mod_f = _ada_mod(c_all, w_ada_f, b_ada_f).reshape(bp + bs, 2, d)
    xp, xs = x_prompt, x_sample
    outs_p, outs_s = [], []
    for l in range(depth):
        mod = _ada_mod(c_all, w_ada[l], b_ada[l]).reshape(bp + bs, N_MOD, d)
        mp, ms = mod[:bp], mod[bp:]
        n_main = 2 * d_rnn + 3 * att_w
        w = {
            "mix_norm": mix_norm[l], "n_heads": n_heads, "q_scale": head_dim ** -0.5 * LOG2_E,
            "w_main": w_in[l, :, :n_main].astype(BF16),
            "w_fgt": _pad_lanes(w_in[l, :, n_main:n_main + n_heads], LANE).astype(BF16),
            "w_gate": w_in[l, :, n_main + n_heads:].astype(BF16),
            "b_fgt": _pad_lanes(b_fgt[l].reshape(1, n_heads), LANE),
            "conv_w": conv_w[l], "conv_b": conv_b[l], "w_gates": _gate_weights(rg_wa[l], rg_wx[l]),
            "rg_ba": rg_ba[l], "rg_bx": rg_bx[l], "rg_lam": rg_lam[l],
            "w_br_rnn": w_br_rnn[l].astype(BF16), "w_br_att": w_br_att[l].astype(BF16),
            "w_out": w_out[l].astype(BF16),
        }
        f1 = (ffn1_norm[l], ffn1_w1[l].astype(BF16), ffn1_w3[l].astype(BF16), ffn1_w2[l].astype(BF16))
        f2 = (ffn2_norm[l], ffn2_w1[l].astype(BF16), ffn2_w3[l].astype(BF16), ffn2_w2[l].astype(BF16))
        last = l == depth - 1

        xp = _ffn(xp, mp, *f1, base=0)
        xs = _ffn(xs, ms, *f1, base=0)

        def attend_prompt(qs, k, v, logf):
            return _attn_prompt(qs, k, v, _cumsum_lanes(jnp.swapaxes(logf, 1, 2)), head_dim)

        def attend_sample(qs, k, v, logf):
            fc_row = _cumsum_lanes(jnp.swapaxes(cache_logf[l], 1, 2))
            fn_row = _cumsum_lanes(_pad_lanes(jnp.swapaxes(logf, 1, 2), LANE))
            fn_col = jnp.swapaxes(fn_row[:, :, :dec_seq], 1, 2)
            return _attn_sample(qs, k, v, cache_k[l].reshape(bs, past, att_w),
                                cache_v[l].reshape(bs, past, att_w), fn_col, fn_row, fc_row, head_dim)

        merge_p, kp, vp, lp, cp, hp = _mixer(xp, mp, w, jnp.zeros((bp, kw - 1, d_rnn), F32),
                                             jnp.zeros((bp, d_rnn), F32), attend_prompt)
        merge_s, ks, vs, ls, cs, hs = _mixer(xs, ms, w, state_conv[l], state_rglru[l], attend_sample)

        xp = _ffn(xp, mp, *f2, base=6, merge=merge_p,
                  final=(final_norm, mod_f[:bp]) if last else None)
        xs = _ffn(xs, ms, *f2, base=6, merge=merge_s,
                  final=(final_norm, mod_f[bp:]) if last else None)
        heads_p = (bp, seq, n_heads, head_dim)
        heads_s = (bs, dec_seq, n_heads, head_dim)
        outs_p.append((kp.reshape(heads_p), vp.reshape(heads_p), lp, cp, hp))
        outs_s.append((ks.reshape(heads_s), vs.reshape(heads_s), ls, cs, hs))
    stack = lambda outs: tuple(jnp.stack(leaf) for leaf in zip(*outs))
    return (xp, xs) + stack(outs_p) + stack(outs_s)
```

```python
import functools
import math

import jax
import jax.numpy as jnp
from jax import lax
from jax.experimental import pallas as pl
from jax.experimental.pallas import tpu as pltpu

F32 = jnp.float32
BF16 = jnp.bfloat16

LANE = 128
SUBLANE = 8
VMEM_BYTES_V7X = 64 * 1024 * 1024
VMEM_LIMIT = VMEM_BYTES_V7X - 8 * 1024 * 1024

EPS = 1e-6
MACARON_W = 0.5
RG_C = 8.0
N_MOD = 9

ROWS_PER_STEP = 512
SCAN_ROWS = 256
ATT_TILE = 256
CUMSUM_CHUNK = 128
CUMSUM_ROWS = 128
NEG_BIG = -1e30
LOG2_E = math.log2(math.e)


def _params(*sem):
    return pltpu.CompilerParams(dimension_semantics=sem, vmem_limit_bytes=VMEM_LIMIT)


def _resident(shape):
    nd = len(shape)
    return pl.BlockSpec(shape, lambda *_: (0,) * nd, pipeline_mode=pl.Buffered(1))


def _row_blocks(b, t, rows):
    if t >= rows:
        assert t % rows == 0, (t, rows)
        return 1, rows
    bb = max(1, min(b, rows // t))
    while b % bb:
        bb -= 1
    return bb, t


def _mod_norm(x, g, shift, scale):
    inv = lax.rsqrt(jnp.mean(x * x, axis=-1, keepdims=True) + EPS)
    return (x * inv) * g * (1.0 + scale) + shift


def _dot(a, b):
    return jnp.dot(a, b, preferred_element_type=F32)


def _log_sigmoid(x):
    return jnp.minimum(x, 0.0) - jnp.log1p(jnp.exp(-jnp.abs(x)))


def _ada_kernel(c_ref, w_ref, b_ref, o_ref):
    c = c_ref[...]
    s = (c * jax.nn.sigmoid(c)).astype(BF16)
    o_ref[...] = _dot(s, w_ref[...].astype(BF16)) + b_ref[...]


def _ada_mod(c, w, b):
    bc, d = c.shape
    n = w.shape[1]
    tn = 1024
    assert n % tn == 0
    return pl.pallas_call(
        _ada_kernel,
        grid=(n // tn,),
        in_specs=[
            pl.BlockSpec((bc, d), lambda j: (0, 0)),
            pl.BlockSpec((d, tn), lambda j: (0, j)),
            pl.BlockSpec((1, tn), lambda j: (0, j)),
        ],
        out_specs=pl.BlockSpec((bc, tn), lambda j: (0, j)),
        out_shape=jax.ShapeDtypeStruct((bc, n), F32),
        compiler_params=_params("parallel"),
        name="ada_mod",
    )(c, w, b.reshape(1, n))


def _ffn_kernel(x_ref, mod_ref, *rest, base, merge, final):
    rest = list(rest)
    x = x_ref[...]
    bb, tm, d = x.shape
    if merge:
        a_ref, t_ref, ga_ref, gb_ref, wr_ref, wa_ref, wo_ref = rest[:7]
        rest = rest[7:]
        rows = bb * tm
        pr = _dot(a_ref[...].reshape(rows, -1), wr_ref[...])
        pa = _dot(t_ref[...].reshape(rows, -1), wa_ref[...])
        z = (jax.nn.sigmoid(ga_ref[...].reshape(rows, d)) * pr
             + jax.nn.sigmoid(gb_ref[...].reshape(rows, d)) * pa)
        x = x + (1.0 + mod_ref[:, 5:6, :]) * _dot(z.astype(BF16), wo_ref[...]).reshape(bb, tm, d)
    g_ref, w1_ref, w3_ref, w2_ref = rest[:4]
    rest = rest[4:]
    if final:
        fg_ref, fmod_ref, o_ref = rest
    else:
        (o_ref,) = rest
    shift = mod_ref[:, base:base + 1, :]
    scale = mod_ref[:, base + 1:base + 2, :]
    gate = 1.0 + mod_ref[:, base + 2:base + 3, :]
    hb = _mod_norm(x, g_ref[...], shift, scale).reshape(bb * tm, d).astype(BF16)
    a = _dot(hb, w1_ref[...])
    b = _dot(hb, w3_ref[...])
    t = ((a * jax.nn.sigmoid(a)) * b).astype(BF16)
    y = _dot(t, w2_ref[...]).reshape(bb, tm, d)
    out = x + MACARON_W * gate * y
    if final:
        out = _mod_norm(out, fg_ref[...], fmod_ref[:, 0:1, :], fmod_ref[:, 1:2, :])
    o_ref[...] = out


def _ffn(x, mod, g, w1, w3, w2, base, merge=None, final=None):
    b, t, d = x.shape
    dff = w1.shape[1]
    bb, tm = _row_blocks(b, t, ROWS_PER_STEP)
    tok = lambda n: pl.BlockSpec((bb, tm, n), lambda i, j: (i, j, 0))
    modspec = lambda n: pl.BlockSpec((bb, n, d), lambda i, j: (i, 0, 0))
    in_specs = [tok(d), modspec(mod.shape[1])]
    args = [x, mod]
    if merge is not None:
        a_out, att, ga, gb, w_br_rnn, w_br_att, w_out = merge
        in_specs += [tok(a_out.shape[-1]), tok(att.shape[-1]), tok(d), tok(d),
                     _resident(w_br_rnn.shape), _resident(w_br_att.shape), _resident(w_out.shape)]
        args += list(merge)
    in_specs += [_resident((1, d)), _resident((d, dff)), _resident((d, dff)), _resident((dff, d))]
    args += [g.reshape(1, d), w1, w3, w2]
    if final is not None:
        fg, fmod = final
        in_specs += [_resident((1, d)), modspec(fmod.shape[1])]
        args += [fg.reshape(1, d), fmod]
    return pl.pallas_call(
        functools.partial(_ffn_kernel, base=base, merge=merge is not None, final=final is not None),
        grid=(b // bb, t // tm),
        in_specs=in_specs,
        out_specs=tok(d),
        out_shape=jax.ShapeDtypeStruct((b, t, d), F32),
        compiler_params=_params("parallel", "parallel"),
        name="ffn" + ("_merge" if merge is not None else "") + ("_final" if final is not None else ""),
    )(*args)


def _inproj_kernel(x_ref, mod_ref, g_ref, wm_ref, wf_ref, wg_ref, bf_ref,
                   xr_o, gr_o, q_o, k_o, v_o, lf_o, ga_o, gb_o, *, q_scale):
    x = x_ref[...]
    bb, tm, d = x.shape
    hb = _mod_norm(x, g_ref[...], mod_ref[:, 3:4, :], mod_ref[:, 4:5, :])
    hb = hb.reshape(bb * tm, d).astype(BF16)

    def proj(w_ref, idx):
        return _dot(hb, w_ref[:, idx * d:(idx + 1) * d]).reshape(bb, tm, d)

    xr_o[...] = proj(wm_ref, 0)
    gr_o[...] = jax.nn.gelu(proj(wm_ref, 1))
    q_o[...] = (proj(wm_ref, 2) * q_scale).astype(q_o.dtype)
    k_o[...] = proj(wm_ref, 3)
    v_o[...] = proj(wm_ref, 4)
    ga_o[...] = proj(wg_ref, 0)
    gb_o[...] = proj(wg_ref, 1)
    n_heads = lf_o.shape[-1]
    lf = _log_sigmoid(_dot(hb, wf_ref[...]) + bf_ref[...])
    lf_o[...] = lf[:, :n_heads].reshape(bb, tm, n_heads)


def _inproj(x, mod, g, w_main, w_fgt, w_gate, b_fgt_pad, n_heads, q_scale):
    b, t, d = x.shape
    bb, tm = _row_blocks(b, t, ROWS_PER_STEP)
    tok = pl.BlockSpec((bb, tm, d), lambda i, j: (i, j, 0))
    f32_out = jax.ShapeDtypeStruct((b, t, d), F32)
    return pl.pallas_call(
        functools.partial(_inproj_kernel, q_scale=q_scale),
        grid=(b // bb, t // tm),
        in_specs=[tok, pl.BlockSpec((bb, mod.shape[1], d), lambda i, j: (i, 0, 0)),
                  _resident((1, d)), _resident(w_main.shape), _resident(w_fgt.shape),
                  _resident(w_gate.shape), _resident((1, LANE))],
        out_specs=[tok, tok, tok, tok, tok,
                   pl.BlockSpec((bb, tm, n_heads), lambda i, j: (i, j, 0)), tok, tok],
        out_shape=[f32_out, f32_out, jax.ShapeDtypeStruct((b, t, d), BF16), f32_out, f32_out,
                   jax.ShapeDtypeStruct((b, t, n_heads), F32), f32_out, f32_out],
        compiler_params=_params("parallel", "parallel"),
        name="in_proj",
    )(x, mod, g.reshape(1, d), w_main, w_fgt, w_gate, b_fgt_pad)


def _linear_scan(a, u, h_in):
    t = a.shape[0]
    n = t // SUBLANE
    a3 = a.reshape(n, SUBLANE, LANE)
    u3 = u.reshape(n, SUBLANE, LANE)
    row = lax.broadcasted_iota(jnp.int32, (n, SUBLANE, LANE), 1)
    for dist in (1, 2, 4):
        keep = row >= dist
        u_prev = pltpu.roll(u3, dist, axis=1)
        a_prev = pltpu.roll(a3, dist, axis=1)
        u3 = jnp.where(keep, a3 * u_prev, 0.0) + u3
        a3 = jnp.where(keep, a3 * a_prev, a3)
    h = h_in
    outs = []
    for grp in range(n):
        hg = u3[grp] + a3[grp] * h
        outs.append(hg)
        h = hg[SUBLANE - 1:SUBLANE, :]
    return jnp.concatenate(outs, axis=0)


def _lru_coefficients(xc, wg_ref, ba_ref, bx_ref, lam_ref):
    a_parts, u_parts = [], []
    for ch in range(xc.shape[1] // LANE):
        sl = slice(ch * LANE, (ch + 1) * LANE)
        xcc = xc[:, sl]
        gates = _dot(xcc.astype(BF16), wg_ref[ch])
        r = jax.nn.sigmoid(gates[:, :LANE] + ba_ref[:, sl])
        i = jax.nn.sigmoid(gates[:, LANE:] + bx_ref[:, sl])
        neg_lam = -lam_ref[:, sl]
        softplus = jnp.maximum(neg_lam, 0.0) + jnp.log1p(jnp.exp(-jnp.abs(neg_lam)))
        log_a = r * ((-RG_C) * softplus)
        a = jnp.exp(log_a)
        a_parts.append(a)
        u_parts.append(jnp.sqrt(-jnp.tanh(log_a) * (a * a + 1.0)) * (i * xcc))
    return a_parts, u_parts


def _rglru_rows(xr_ref, gr_ref, cw_ref, cb_ref, coeff_refs, aout_ref, buf, hcar):
    tc = xr_ref.shape[1]
    kw = cw_ref.shape[0]
    lo = SUBLANE - (kw - 1)
    buf[SUBLANE:SUBLANE + tc, :] = xr_ref[0]
    xc = cb_ref[...]
    for j in range(kw):
        xc = xc + buf[lo + j:lo + j + tc, :] * cw_ref[j:j + 1, :]
    buf[lo:SUBLANE, :] = buf[SUBLANE + tc - (kw - 1):SUBLANE + tc, :]
    a_parts, u_parts = _lru_coefficients(xc, *coeff_refs)
    for ch, (a, u) in enumerate(zip(a_parts, u_parts)):
        sl = slice(ch * LANE, (ch + 1) * LANE)
        hs = _linear_scan(a, u, hcar[:, sl])
        hcar[:, sl] = hs[tc - 1:tc, :]
        aout_ref[0, :, sl] = (gr_ref[0, :, sl] * hs).astype(aout_ref.dtype)


def _rglru_segments(xr_ref, gr_ref, cw_ref, cb_ref, coeff_refs, aout_ref, buf, hcar, xs, hs_s):
    tc, c = xr_ref.shape[1], xr_ref.shape[2]
    kw = cw_ref.shape[0]
    seg = tc // SUBLANE
    n_ch = c // LANE
    sub = lax.broadcasted_iota(jnp.int32, (SUBLANE, c), 0)

    for s in range(SUBLANE):
        for ch in range(n_ch):
            xs[ch, pl.ds(s, seg, stride=SUBLANE), :] = xr_ref[0, s * seg:(s + 1) * seg,
                                                             ch * LANE:(ch + 1) * LANE]
    x = jnp.concatenate([xs[ch] for ch in range(n_ch)], axis=1)

    wrapped = []
    for i in range(kw - 1):
        p = seg - (kw - 1) + i
        cur = x[p * SUBLANE:(p + 1) * SUBLANE, :]
        prev = buf[i * SUBLANE:(i + 1) * SUBLANE, :]
        wrapped.append(pltpu.roll(jnp.where(sub == SUBLANE - 1, prev, cur), 1, axis=0))
    buf[...] = x[(seg - (kw - 1)) * SUBLANE:, :]
    xc = cb_ref[...] + x * cw_ref[kw - 1:kw, :]
    for j in range(1, kw):
        shifted = jnp.concatenate(wrapped[kw - 1 - j:] + [x[:(seg - j) * SUBLANE, :]], axis=0)
        xc = xc + shifted * cw_ref[kw - 1 - j:kw - j, :]

    a_parts, u_parts = _lru_coefficients(xc, *coeff_refs)
    a = jnp.concatenate(a_parts, axis=1)
    u = jnp.concatenate(u_parts, axis=1)

    h = u[:SUBLANE, :]
    prod = a[:SUBLANE, :]
    local, prods = [h], [prod]
    for p in range(1, seg):
        a_p = a[p * SUBLANE:(p + 1) * SUBLANE, :]
        h = a_p * h + u[p * SUBLANE:(p + 1) * SUBLANE, :]
        prod = a_p * prod
        local.append(h)
        prods.append(prod)
    state = hcar[...]
    entering = []
    for s in range(SUBLANE):
        entering.append(state)
        state = h[s:s + 1, :] + prod[s:s + 1, :] * state
    hcar[...] = state
    h_in = jnp.concatenate(entering, axis=0)
    for p in range(seg):
        full = local[p] + prods[p] * h_in
        for ch in range(n_ch):
            hs_s[ch, p * SUBLANE:(p + 1) * SUBLANE, :] = full[:, ch * LANE:(ch + 1) * LANE]

    for ch in range(n_ch):
        sl = slice(ch * LANE, (ch + 1) * LANE)
        hn = jnp.concatenate([hs_s[ch, pl.ds(s, seg, stride=SUBLANE), :] for s in range(SUBLANE)], axis=0)
        aout_ref[0, :, sl] = (gr_ref[0, :, sl] * hn).astype(aout_ref.dtype)


def _rglru_kernel(xr_ref, gr_ref, conv0_ref, h0_ref, cw_ref, cb_ref, wg_ref, ba_ref, bx_ref, lam_ref,
                  aout_ref, convo_ref, hlast_ref, buf, hcar, *stage):
    tc = xr_ref.shape[1]
    kw = cw_ref.shape[0]
    segmented = bool(stage)

    @pl.when(pl.program_id(1) == 0)
    def _():
        hcar[...] = h0_ref[0]
        if segmented:
            buf[...] = jnp.zeros(buf.shape, F32)
            for i in range(kw - 1):
                buf[i * SUBLANE + SUBLANE - 1:(i + 1) * SUBLANE, :] = conv0_ref[0, i:i + 1, :]
        else:
            buf[SUBLANE - (kw - 1):SUBLANE, :] = conv0_ref[0]

    coeff_refs = (wg_ref, ba_ref, bx_ref, lam_ref)
    if segmented:
        _rglru_segments(xr_ref, gr_ref, cw_ref, cb_ref, coeff_refs, aout_ref, buf, hcar, *stage)
    else:
        _rglru_rows(xr_ref, gr_ref, cw_ref, cb_ref, coeff_refs, aout_ref, buf, hcar)
    convo_ref[0] = xr_ref[0, tc - (kw - 1):tc, :]
    hlast_ref[0] = hcar[...]


def _rglru(xr, gr, conv0, h0, conv_w, conv_b, w_gates, ba, bx, lam):
    b, t, c = xr.shape
    kw = conv_w.shape[0]
    tc = min(t, SCAN_ROWS)
    assert t % tc == 0 and tc % SUBLANE == 0 and tc >= kw - 1 and kw - 1 <= SUBLANE
    tok = pl.BlockSpec((1, tc, c), lambda i, j: (i, j, 0))
    per_b = lambda n: pl.BlockSpec((1, n, c), lambda i, j: (i, 0, 0))
    vec = _resident((1, c))
    if tc // SUBLANE >= SUBLANE:
        staging = pltpu.VMEM((c // LANE, tc, LANE), F32)
        scratch = [pltpu.VMEM(((kw - 1) * SUBLANE, c), F32), pltpu.VMEM((1, c), F32), staging, staging]
    else:
        scratch = [pltpu.VMEM((SUBLANE + tc, c), F32), pltpu.VMEM((1, c), F32)]
    return pl.pallas_call(
        _rglru_kernel,
        grid=(b, t // tc),
        in_specs=[tok, tok, per_b(kw - 1), per_b(1), _resident((kw, c)), vec,
                  _resident(w_gates.shape), vec, vec, vec],
        out_specs=[tok, per_b(kw - 1), per_b(1)],
        out_shape=[jax.ShapeDtypeStruct((b, t, c), BF16),
                   jax.ShapeDtypeStruct((b, kw - 1, c), F32),
                   jax.ShapeDtypeStruct((b, 1, c), F32)],
        scratch_shapes=scratch,
        compiler_params=_params("parallel", "arbitrary"),
        name="rglru",
    )(xr, gr, conv0, h0.reshape(b, 1, c), conv_w, conv_b.reshape(1, c), w_gates,
      ba.reshape(1, c), bx.reshape(1, c), lam.reshape(1, c))


def _cumsum_kernel(x_ref, o_ref):
    rows, t = x_ref.shape
    r = lax.broadcasted_iota(jnp.int32, (CUMSUM_CHUNK, CUMSUM_CHUNK), 0)
    c = lax.broadcasted_iota(jnp.int32, (CUMSUM_CHUNK, CUMSUM_CHUNK), 1)
    upper = (r <= c).astype(F32)
    carry = jnp.zeros((rows, 1), F32)
    for i in range(t // CUMSUM_CHUNK):
        sl = slice(i * CUMSUM_CHUNK, (i + 1) * CUMSUM_CHUNK)
        cs = jnp.dot(x_ref[:, sl], upper, preferred_element_type=F32,
                     precision=lax.Precision.HIGHEST) + carry
        o_ref[:, sl] = cs
        carry = cs[:, CUMSUM_CHUNK - 1:CUMSUM_CHUNK]


def _cumsum_lanes(x):
    b, h, t = x.shape
    n = b * h
    rows = min(n, CUMSUM_ROWS)
    assert t % CUMSUM_CHUNK == 0 and n % rows == 0
    spec = pl.BlockSpec((rows, t), lambda i: (i, 0))
    return pl.pallas_call(
        _cumsum_kernel, grid=(n // rows,), in_specs=[spec], out_specs=spec,
        out_shape=jax.ShapeDtypeStruct((n, t), F32),
        compiler_params=_params("parallel"), name="fgt_cumsum",
    )(x.reshape(n, t)).reshape(b, h, t)


def _head_column(f_blk, head):
    lane = lax.broadcasted_iota(jnp.int32, f_blk.shape, 1)
    return jnp.sum(jnp.where(lane == head, f_blk, 0.0), axis=-1, keepdims=True)


def _softmax_step(qm, kt, vt, bias, carry, mask=None):
    m_i, l_i, acc = carry
    s = lax.dot_general(qm, kt, (((1,), (1,)), ((), ())), preferred_element_type=F32) + bias
    if mask is not None:
        s = jnp.where(mask, s, -jnp.inf)
    m_new = jnp.maximum(m_i, jnp.max(s, axis=-1, keepdims=True))
    alpha = jnp.exp2(m_i - m_new)
    p = jnp.exp2(s - m_new)
    l_new = alpha * l_i + jnp.sum(p, axis=-1, keepdims=True)
    acc = alpha * acc + _dot(p.astype(BF16), vt)
    return m_new, l_new, acc


def _split3(x):
    hi = x.astype(BF16).astype(F32)
    rest = x - hi
    mid = rest.astype(BF16).astype(F32)
    lo = (rest - mid).astype(BF16).astype(F32)
    return hi, mid, lo


def _bias_lanes(f_rows):
    heads, t = f_rows.shape
    n = 3 * heads
    terms = _split3(f_rows)
    rows = 2 * SUBLANE
    assert 2 * n <= rows
    row = lax.broadcasted_iota(jnp.int32, (rows, t), 0)
    packed = jnp.zeros((rows, t), F32)
    for hh in range(heads):
        for j in range(3):
            term = terms[j][hh:hh + 1, :]
            packed = jnp.where((row == 3 * hh + j) | (row == n + 3 * hh + j), term, packed)
    cols = jnp.concatenate([packed, jnp.zeros((LANE - rows, t), F32)], axis=0).T
    lane = lax.broadcasted_iota(jnp.int32, (1, LANE), 1)
    key_lanes = jnp.where(lane < n, -cols, jnp.where(lane < 2 * n, 1.0, 0.0))
    query_lanes = jnp.where(lane < n, 1.0, jnp.where(lane < 2 * n, cols, 0.0))
    return key_lanes, query_lanes


def _qk_scores(kt, queries):
    return [lax.dot_general(kt, qa, (((1,), (1,)), ((), ())), preferred_element_type=F32)
            for qa in queries]


def _softmax_pv(scores, values, states, masks):
    partial = []
    for s, (m_i, l_i, acc), mask in zip(scores, states, masks):
        if mask is not None:
            s = jnp.where(mask, s, -jnp.inf)
        m_new = jnp.maximum(m_i, jnp.max(s, axis=0, keepdims=True))
        alpha = jnp.exp2(m_i - m_new)
        p = jnp.exp2(s - m_new)
        l_new = alpha * l_i + jnp.sum(p, axis=0, keepdims=True)
        partial.append((m_new, l_new, alpha, p.astype(BF16)))
    return [(m_new, l_new, alpha * acc + _dot(vt, p))
            for (m_new, l_new, alpha, p), (_, _, acc), vt in zip(partial, states, values)]


def _merge_softmax(a, b):
    (m_a, l_a, acc_a), (m_b, l_b, acc_b) = a, b
    m = jnp.maximum(m_a, m_b)
    w_a = jnp.exp2(m_a - m)
    w_b = jnp.exp2(m_b - m)
    return m, w_a * l_a + w_b * l_b, w_a * acc_a + w_b * acc_b


def _attn_prompt_kernel(q_ref, k_ref, v_ref, f_ref, o_ref, kaug, qext, vt_s, *, head_dim):
    t = q_ref.shape[1]
    tq = ATT_TILE
    heads = LANE // head_dim
    lane = lax.broadcasted_iota(jnp.int32, (1, LANE), 1)
    sub = lax.broadcasted_iota(jnp.int32, (LANE, 1), 0)
    key_pos = lax.broadcasted_iota(jnp.int32, (tq, tq), 0)
    qry_pos = lax.broadcasted_iota(jnp.int32, (tq, tq), 1)
    causal = key_pos <= qry_pos

    key_lanes, query_lanes = _bias_lanes(f_ref[0, 0] * LOG2_E)
    kaug[:, :LANE] = k_ref[0].astype(BF16)
    kaug[:, LANE:] = key_lanes.astype(BF16)
    qext[...] = query_lanes.astype(BF16)
    vt_s[...] = v_ref[0].T.astype(BF16)

    def query_tile(q0):
        q = q_ref[0, pl.ds(q0, tq), :]
        extra = qext[pl.ds(q0, tq), :]
        out = []
        for hh in range(heads):
            in_head = (lane >= hh * head_dim) & (lane < (hh + 1) * head_dim)
            lo = 3 * hh
            hi = 3 * heads + 3 * hh
            own = ((lane >= lo) & (lane < lo + 3)) | ((lane >= hi) & (lane < hi + 3))
            out.append(jnp.concatenate([jnp.where(in_head, q, jnp.zeros_like(q)),
                                        jnp.where(own, extra, jnp.zeros_like(extra))], axis=1))
        return out

    def init_state():
        return (jnp.full((1, tq), NEG_BIG, F32), jnp.zeros((1, tq), F32), jnp.zeros((LANE, tq), F32))

    def finish(q0, states):
        out_t = jnp.zeros((LANE, tq), F32)
        for hh, (_, l_i, acc) in enumerate(states):
            in_head = (sub >= hh * head_dim) & (sub < (hh + 1) * head_dim)
            out_t = jnp.where(in_head, acc / l_i, out_t)
        o_ref[0, pl.ds(q0, tq), :] = out_t.T.astype(o_ref.dtype)

    def keys(k0):
        return kaug[pl.ds(k0, tq), :]

    def values(k0):
        return vt_s[:, pl.ds(k0, tq)]

    n_chain = 2 * heads
    unmasked = [None] * n_chain

    def pair_queries(a):
        return query_tile(2 * a * tq) + query_tile((2 * a + 1) * tq)

    n_pairs = t // (2 * tq)
    queries = pair_queries(0)
    scores = _qk_scores(keys(0), queries)
    for a in range(n_pairs):
        states = [init_state() for _ in range(n_chain)]
        for j in range(2 * a):
            nxt = _qk_scores(keys((j + 1) * tq), queries)
            states = _softmax_pv(scores, [values(j * tq)] * n_chain, states, unmasked)
            scores = nxt
        q_lo, q_hi = 2 * a * tq, (2 * a + 1) * tq
        diagonal = scores + _qk_scores(keys(q_hi), queries[heads:])
        if a + 1 < n_pairs:
            queries = pair_queries(a + 1)
            scores = _qk_scores(keys(0), queries)
        states = _softmax_pv(
            diagonal,
            [values(q_lo)] * n_chain + [values(q_hi)] * heads,
            states + [init_state() for _ in range(heads)],
            [causal] * heads + [None] * heads + [causal] * heads)
        finish(q_lo, states[:heads])
        finish(q_hi, [_merge_softmax(states[heads + hh], states[n_chain + hh]) for hh in range(heads)])


def _attn_prompt(qs, k, v, f_row, head_dim):
    b, t, w = qs.shape
    h = f_row.shape[1]
    heads = LANE // head_dim
    assert LANE % head_dim == 0 and w % LANE == 0 and t % (2 * ATT_TILE) == 0 and h % heads == 0
    tok = pl.BlockSpec((1, t, LANE), lambda i, j: (i, 0, j))
    return pl.pallas_call(
        functools.partial(_attn_prompt_kernel, head_dim=head_dim),
        grid=(b, w // LANE),
        in_specs=[tok, tok, tok, pl.BlockSpec((1, 1, heads, t), lambda i, j: (i, j, 0, 0))],
        out_specs=tok,
        out_shape=jax.ShapeDtypeStruct((b, t, w), BF16),
        scratch_shapes=[pltpu.VMEM((t, 2 * LANE), BF16), pltpu.VMEM((t, LANE), BF16),
                        pltpu.VMEM((LANE, t), BF16)],
        compiler_params=_params("parallel", "arbitrary"),
        name="attn_prompt",
    )(qs, k, v, f_row.reshape(b, h // heads, heads, t))


def _attn_sample_kernel(q_ref, kn_ref, vn_ref, ck_ref, cv_ref, fn_ref, fnt_ref, fct_ref, o_ref,
                        kall, vall, *, head_dim):
    ts, past = q_ref.shape[1], ck_ref.shape[1]
    heads_per_block = LANE // head_dim
    pair = pl.program_id(1)
    kall[0:past, :] = ck_ref[0].astype(BF16)
    vall[0:past, :] = cv_ref[0].astype(BF16)
    kall[past:, :] = jnp.zeros((LANE, LANE), BF16)
    vall[past:, :] = jnp.zeros((LANE, LANE), BF16)
    kall[past:past + ts, :] = kn_ref[0].astype(BF16)
    vall[past:past + ts, :] = vn_ref[0].astype(BF16)
    lane = lax.broadcasted_iota(jnp.int32, (1, LANE), 1)
    row = lax.broadcasted_iota(jnp.int32, (ts, past + LANE), 0)
    col = lax.broadcasted_iota(jnp.int32, (ts, past + LANE), 1)
    visible = col <= past + row
    q = q_ref[0]
    out = jnp.zeros((ts, LANE), F32)
    for hh in range(heads_per_block):
        head = pair * heads_per_block + hh
        in_head = (lane >= hh * head_dim) & (lane < (hh + 1) * head_dim)
        qm = jnp.where(in_head, q, jnp.zeros_like(q))
        f_cache = fct_ref[0, pl.ds(head, 1), :]
        total = f_cache[:, past - 1:past]
        f_keys = jnp.concatenate([f_cache, total + fnt_ref[0, pl.ds(head, 1), :]], axis=1)
        bias = ((total + _head_column(fn_ref[0], head)) - f_keys) * LOG2_E
        init = (jnp.full((ts, 1), NEG_BIG, F32), jnp.zeros((ts, 1), F32), jnp.zeros((ts, LANE), F32))
        _, l_i, acc = _softmax_step(qm, kall[...], vall[...], bias, init, mask=visible)
        out = jnp.where(in_head, acc / l_i, out)
    o_ref[0] = out.astype(o_ref.dtype)


def _attn_sample(qs, k_new, v_new, cache_k, cache_v, fn_col, fn_row, fc_row, head_dim):
    b, ts, w = qs.shape
    past = cache_k.shape[1]
    h = fn_col.shape[-1]
    assert ts <= LANE and past % LANE == 0
    new = pl.BlockSpec((1, ts, LANE), lambda i, j: (i, 0, j))
    old = pl.BlockSpec((1, past, LANE), lambda i, j: (i, 0, j))
    return pl.pallas_call(
        functools.partial(_attn_sample_kernel, head_dim=head_dim),
        grid=(b, w // LANE),
        in_specs=[new, new, new, old, old,
                  pl.BlockSpec((1, ts, h), lambda i, j: (i, 0, 0)),
                  pl.BlockSpec((1, h, LANE), lambda i, j: (i, 0, 0)),
                  pl.BlockSpec((1, h, past), lambda i, j: (i, 0, 0))],
        out_specs=new,
        out_shape=jax.ShapeDtypeStruct((b, ts, w), BF16),
        scratch_shapes=[pltpu.VMEM((past + LANE, LANE), BF16), pltpu.VMEM((past + LANE, LANE), BF16)],
        compiler_params=_params("parallel", "arbitrary"),
        name="attn_sample",
    )(qs, k_new, v_new, cache_k, cache_v, fn_col, fn_row, fc_row)


def _gate_weights(wa, wx):
    nb, bw, _ = wa.shape
    per = LANE // bw

    def chunked(w):
        w = w.reshape(nb // per, per, bw, bw)
        eye = jnp.eye(per, dtype=w.dtype)
        return jnp.einsum("cpij,pq->cpiqj", w, eye).reshape(nb // per, LANE, LANE)

    return jnp.concatenate([chunked(wa), chunked(wx)], axis=-1).astype(BF16)


def _pad_lanes(x, n):
    return jnp.pad(x, [(0, 0)] * (x.ndim - 1) + [(0, n - x.shape[-1])])


def _mixer(x, mod, w, conv0, h0, attend):
    b, t, d = x.shape
    xr, gr, qs, k, v, logf, ga, gb = _inproj(
        x, mod, w["mix_norm"], w["w_main"], w["w_fgt"], w["w_gate"], w["b_fgt"],
        w["n_heads"], w["q_scale"])
    a_out, conv_state, h_last = _rglru(xr, gr, conv0, h0, w["conv_w"], w["conv_b"], w["w_gates"],
                                       w["rg_ba"], w["rg_bx"], w["rg_lam"])
    att = attend(qs, k, v, logf)
    merge = (a_out, att, ga, gb, w["w_br_rnn"], w["w_br_att"], w["w_out"])
    return merge, k, v, logf, conv_state, h_last.reshape(b, -1)


def kernel(x_prompt, x_sample, cache_k, cache_v, cache_logf, state_conv, state_rglru, c_prompt, c_sample, w_ada, b_ada, ffn1_norm, ffn1_w1, ffn1_w3, ffn1_w2, mix_norm, w_in, b_fgt, conv_w, conv_b, rg_wa, rg_ba, rg_wx, rg_bx, rg_lam, w_br_rnn, w_br_att, w_out, ffn2_norm, ffn2_w1, ffn2_w3, ffn2_w2, final_norm, w_ada_f, b_ada_f):
    bp, seq, d = x_prompt.shape
    bs, dec_seq, _ = x_sample.shape
    depth = w_ada.shape[0]
    n_heads = b_fgt.shape[-1]
    head_dim = cache_k.shape[-1]
    d_rnn = conv_w.shape[-1]
    kw = conv_w.shape[1]
    att_w = n_heads * head_dim
    past = cache_k.shape[2]
    assert d_rnn == d and att_w == d and n_heads <= LANE
    assert w_in.shape[-1] == 2 * d_rnn + 3 * att_w + n_heads + 2 * d

    c_all = jnp.concatenate([c_prompt, c_sample], axis=0)
    mod_f = _ada_mod(c_all, w_ada_f, b_ada_f).reshape(bp + bs, 2, d)
    xp, xs = x_prompt, x_sample
    outs_p, outs_s = [], []
    for l in range(depth):
        mod = _ada_mod(c_all, w_ada[l], b_ada[l]).reshape(bp + bs, N_MOD, d)
        mp, ms = mod[:bp], mod[bp:]
        n_main = 2 * d_rnn + 3 * att_w
        w = {
            "mix_norm": mix_norm[l], "n_heads": n_heads, "q_scale": head_dim ** -0.5 * LOG2_E,
            "w_main": w_in[l, :, :n_main].astype(BF16),
            "w_fgt": _pad_lanes(w_in[l, :, n_main:n_main + n_heads], LANE).astype(BF16),
            "w_gate": w_in[l, :, n_main + n_heads:].astype(BF16),
            "b_fgt": _pad_lanes(b_fgt[l].reshape(1, n_heads), LANE),
            "conv_w": conv_w[l], "conv_b": conv_b[l], "w_gates": _gate_weights(rg_wa[l], rg_wx[l]),
            "rg_ba": rg_ba[l], "rg_bx": rg_bx[l], "rg_lam": rg_lam[l],
            "w_br_rnn": w_br_rnn[l].astype(BF16), "w_br_att": w_br_att[l].astype(BF16),
            "w_out": w_out[l].astype(BF16),
        }
        f1 = (ffn1_norm[l], ffn1_w1[l].astype(BF16), ffn1_w3[l].astype(BF16), ffn1_w2[l].astype(BF16))
        f2 = (ffn2_norm[l], ffn2_w1[l].astype(BF16), ffn2_w3[l].astype(BF16), ffn2_w2[l].astype(BF16))
        last = l == depth - 1

        xp = _ffn(xp, mp, *f1, base=0)
        xs = _ffn(xs, ms, *f1, base=0)

        def attend_prompt(qs, k, v, logf):
            return _attn_prompt(qs, k, v, _cumsum_lanes(jnp.swapaxes(logf, 1, 2)), head_dim)

        def attend_sample(qs, k, v, logf):
            fc_row = _cumsum_lanes(jnp.swapaxes(cache_logf[l], 1, 2))
            fn_row = _cumsum_lanes(_pad_lanes(jnp.swapaxes(logf, 1, 2), LANE))
            fn_col = jnp.swapaxes(fn_row[:, :, :dec_seq], 1, 2)
            return _attn_sample(qs, k, v, cache_k[l].reshape(bs, past, att_w),
                                cache_v[l].reshape(bs, past, att_w), fn_col, fn_row, fc_row, head_dim)

        merge_p, kp, vp, lp, cp, hp = _mixer(xp, mp, w, jnp.zeros((bp, kw - 1, d_rnn), F32),
                                             jnp.zeros((bp, d_rnn), F32), attend_prompt)
        merge_s, ks, vs, ls, cs, hs = _mixer(xs, ms, w, state_conv[l], state_rglru[l], attend_sample)

        xp = _ffn(xp, mp, *f2, base=6, merge=merge_p,
                  final=(final_norm, mod_f[:bp]) if last else None)
        xs = _ffn(xs, ms, *f2, base=6, merge=merge_s,
                  final=(final_norm, mod_f[bp:]) if last else None)
        heads_p = (bp, seq, n_heads, head_dim)
        heads_s = (bs, dec_seq, n_heads, head_dim)
        outs_p.append((kp.reshape(heads_p), vp.reshape(heads_p), lp, cp, hp))
        outs_s.append((ks.reshape(heads_s), vs.reshape(heads_s), ls, cs, hs))
    stack = lambda outs: tuple(jnp.stack(leaf) for leaf in zip(*outs))
    return (xp, xs) + stack(outs_p) + stack(outs_s)
```

```python
import functools
import math

import jax
import jax.numpy as jnp
from jax import lax
from jax.experimental import pallas as pl
from jax.experimental.pallas import tpu as pltpu

F32 = jnp.float32
BF16 = jnp.bfloat16

LANE = 128
SUBLANE = 8
VMEM_BYTES_V7X = 64 * 1024 * 1024
VMEM_LIMIT = VMEM_BYTES_V7X - 8 * 1024 * 1024

EPS = 1e-6
MACARON_W = 0.5
RG_C = 8.0
N_MOD = 9

ROWS_PER_STEP = 512
SCAN_ROWS = 256
ATT_TILE = 256
CUMSUM_CHUNK = 128
CUMSUM_ROWS = 128
NEG_BIG = -1e30
LOG2_E = math.log2(math.e)


def _params(*sem):
    return pltpu.CompilerParams(dimension_semantics=sem, vmem_limit_bytes=VMEM_LIMIT)


def _resident(shape):
    nd = len(shape)
    return pl.BlockSpec(shape, lambda *_: (0,) * nd, pipeline_mode=pl.Buffered(1))


def _row_blocks(b, t, rows):
    if t >= rows:
        assert t % rows == 0, (t, rows)
        return 1, rows
    bb = max(1, min(b, rows // t))
    while b % bb:
        bb -= 1
    return bb, t


def _mod_norm(x, g, shift, scale):
    inv = lax.rsqrt(jnp.mean(x * x, axis=-1, keepdims=True) + EPS)
    return (x * inv) * g * (1.0 + scale) + shift


def _dot(a, b):
    return jnp.dot(a, b, preferred_element_type=F32)


def _log_sigmoid(x):
    return jnp.minimum(x, 0.0) - jnp.log1p(jnp.exp(-jnp.abs(x)))


def _ada_kernel(c_ref, w_ref, b_ref, o_ref):
    c = c_ref[...]
    s = (c * jax.nn.sigmoid(c)).astype(BF16)
    o_ref[...] = _dot(s, w_ref[...].astype(BF16)) + b_ref[...]


def _ada_mod(c, w, b):
    bc, d = c.shape
    n = w.shape[1]
    tn = 1024
    assert n % tn == 0
    return pl.pallas_call(
        _ada_kernel,
        grid=(n // tn,),
        in_specs=[
            pl.BlockSpec((bc, d), lambda j: (0, 0)),
            pl.BlockSpec((d, tn), lambda j: (0, j)),
            pl.BlockSpec((1, tn), lambda j: (0, j)),
        ],
        out_specs=pl.BlockSpec((bc, tn), lambda j: (0, j)),
        out_shape=jax.ShapeDtypeStruct((bc, n), F32),
        compiler_params=_params("parallel"),
        name="ada_mod",
    )(c, w, b.reshape(1, n))


def _ffn_kernel(x_ref, mod_ref, *rest, base, merge, final):
    rest = list(rest)
    bb, tm, d = x_ref.shape
    n_part = 2 if tm % (2 * SUBLANE) == 0 else 1
    tp = tm // n_part
    rows = bb * tp

    def parts(ref):
        return [ref[:, i * tp:(i + 1) * tp, :] for i in range(n_part)]

    xs = parts(x_ref)
    if merge:
        a_ref, t_ref, ga_ref, gb_ref, wr_ref, wa_ref, wo_ref = rest[:7]
        rest = rest[7:]
        prs = [_dot(a.reshape(rows, -1), wr_ref[...]) for a in parts(a_ref)]
        pas = [_dot(t.reshape(rows, -1), wa_ref[...]) for t in parts(t_ref)]
        zs = [(jax.nn.sigmoid(ga.reshape(rows, d)) * pr
               + jax.nn.sigmoid(gb.reshape(rows, d)) * pa).astype(BF16)
              for ga, gb, pr, pa in zip(parts(ga_ref), parts(gb_ref), prs, pas)]
        out_gate = 1.0 + mod_ref[:, 5:6, :]
        xs = [x + out_gate * _dot(z, wo_ref[...]).reshape(bb, tp, d) for x, z in zip(xs, zs)]
    g_ref, w1_ref, w3_ref, w2_ref = rest[:4]
    rest = rest[4:]
    if final:
        fg_ref, fmod_ref, o_ref = rest
    else:
        (o_ref,) = rest
    shift = mod_ref[:, base:base + 1, :]
    scale = mod_ref[:, base + 1:base + 2, :]
    gate = 1.0 + mod_ref[:, base + 2:base + 3, :]
    hbs = [_mod_norm(x, g_ref[...], shift, scale).reshape(rows, d).astype(BF16) for x in xs]
    ups = [(_dot(hb, w1_ref[...]), _dot(hb, w3_ref[...])) for hb in hbs]
    ts = [((a * jax.nn.sigmoid(a)) * b).astype(BF16) for a, b in ups]
    ys = [_dot(t, w2_ref[...]).reshape(bb, tp, d) for t in ts]
    for i, (x, y) in enumerate(zip(xs, ys)):
        out = x + MACARON_W * gate * y
        if final:
            out = _mod_norm(out, fg_ref[...], fmod_ref[:, 0:1, :], fmod_ref[:, 1:2, :])
        o_ref[:, i * tp:(i + 1) * tp, :] = out


def _ffn(x, mod, g, w1, w3, w2, base, merge=None, final=None):
    b, t, d = x.shape
    dff = w1.shape[1]
    bb, tm = _row_blocks(b, t, ROWS_PER_STEP)
    tok = lambda n: pl.BlockSpec((bb, tm, n), lambda i, j: (i, j, 0))
    modspec = lambda n: pl.BlockSpec((bb, n, d), lambda i, j: (i, 0, 0))
    in_specs = [tok(d), modspec(mod.shape[1])]
    args = [x, mod]
    if merge is not None:
        a_out, att, ga, gb, w_br_rnn, w_br_att, w_out = merge
        in_specs += [tok(a_out.shape[-1]), tok(att.shape[-1]), tok(d), tok(d),
                     _resident(w_br_rnn.shape), _resident(w_br_att.shape), _resident(w_out.shape)]
        args += list(merge)
    in_specs += [_resident((1, d)), _resident((d, dff)), _resident((d, dff)), _resident((dff, d))]
    args += [g.reshape(1, d), w1, w3, w2]
    if final is not None:
        fg, fmod = final
        in_specs += [_resident((1, d)), modspec(fmod.shape[1])]
        args += [fg.reshape(1, d), fmod]
    return pl.pallas_call(
        functools.partial(_ffn_kernel, base=base, merge=merge is not None, final=final is not None),
        grid=(b // bb, t // tm),
        in_specs=in_specs,
        out_specs=tok(d),
        out_shape=jax.ShapeDtypeStruct((b, t, d), F32),
        compiler_params=_params("parallel", "parallel"),
        name="ffn" + ("_merge" if merge is not None else "") + ("_final" if final is not None else ""),
    )(*args)


def _inproj_kernel(x_ref, mod_ref, g_ref, wm_ref, wf_ref, wg_ref, bf_ref,
                   xr_o, gr_o, q_o, k_o, v_o, lf_o, ga_o, gb_o, *, q_scale):
    x = x_ref[...]
    bb, tm, d = x.shape
    hb = _mod_norm(x, g_ref[...], mod_ref[:, 3:4, :], mod_ref[:, 4:5, :])
    hb = hb.reshape(bb * tm, d).astype(BF16)

    def proj(w_ref, idx):
        return _dot(hb, w_ref[:, idx * d:(idx + 1) * d]).reshape(bb, tm, d)

    xr_o[...] = proj(wm_ref, 0)
    gr_o[...] = jax.nn.gelu(proj(wm_ref, 1))
    q_o[...] = (proj(wm_ref, 2) * q_scale).astype(q_o.dtype)
    k_o[...] = proj(wm_ref, 3)
    v_o[...] = proj(wm_ref, 4)
    ga_o[...] = proj(wg_ref, 0)
    gb_o[...] = proj(wg_ref, 1)
    n_heads = lf_o.shape[-1]
    lf = _log_sigmoid(_dot(hb, wf_ref[...]) + bf_ref[...])
    lf_o[...] = lf[:, :n_heads].reshape(bb, tm, n_heads)


def _inproj(x, mod, g, w_main, w_fgt, w_gate, b_fgt_pad, n_heads, q_scale):
    b, t, d = x.shape
    bb, tm = _row_blocks(b, t, ROWS_PER_STEP)
    tok = pl.BlockSpec((bb, tm, d), lambda i, j: (i, j, 0))
    f32_out = jax.ShapeDtypeStruct((b, t, d), F32)
    return pl.pallas_call(
        functools.partial(_inproj_kernel, q_scale=q_scale),
        grid=(b // bb, t // tm),
        in_specs=[tok, pl.BlockSpec((bb, mod.shape[1], d), lambda i, j: (i, 0, 0)),
                  _resident((1, d)), _resident(w_main.shape), _resident(w_fgt.shape),
                  _resident(w_gate.shape), _resident((1, LANE))],
        out_specs=[tok, tok, tok, tok, tok,
                   pl.BlockSpec((bb, tm, n_heads), lambda i, j: (i, j, 0)), tok, tok],
        out_shape=[f32_out, f32_out, jax.ShapeDtypeStruct((b, t, d), BF16), f32_out, f32_out,
                   jax.ShapeDtypeStruct((b, t, n_heads), F32), f32_out, f32_out],
        compiler_params=_params("parallel", "parallel"),
        name="in_proj",
    )(x, mod, g.reshape(1, d), w_main, w_fgt, w_gate, b_fgt_pad)


def _linear_scan(a, u, h_in):
    t = a.shape[0]
    n = t // SUBLANE
    a3 = a.reshape(n, SUBLANE, LANE)
    u3 = u.reshape(n, SUBLANE, LANE)
    row = lax.broadcasted_iota(jnp.int32, (n, SUBLANE, LANE), 1)
    for dist in (1, 2, 4):
        keep = row >= dist
        u_prev = pltpu.roll(u3, dist, axis=1)
        a_prev = pltpu.roll(a3, dist, axis=1)
        u3 = jnp.where(keep, a3 * u_prev, 0.0) + u3
        a3 = jnp.where(keep, a3 * a_prev, a3)
    h = h_in
    outs = []
    for grp in range(n):
        hg = u3[grp] + a3[grp] * h
        outs.append(hg)
        h = hg[SUBLANE - 1:SUBLANE, :]
    return jnp.concatenate(outs, axis=0)


def _lru_coefficients(xc, wg_ref, ba_ref, bx_ref, lam_ref):
    a_parts, u_parts = [], []
    for ch in range(xc.shape[1] // LANE):
        sl = slice(ch * LANE, (ch + 1) * LANE)
        xcc = xc[:, sl]
        gates = _dot(xcc.astype(BF16), wg_ref[ch])
        r = jax.nn.sigmoid(gates[:, :LANE] + ba_ref[:, sl])
        i = jax.nn.sigmoid(gates[:, LANE:] + bx_ref[:, sl])
        neg_lam = -lam_ref[:, sl]
        softplus = jnp.maximum(neg_lam, 0.0) + jnp.log1p(jnp.exp(-jnp.abs(neg_lam)))
        log_a = r * ((-RG_C) * softplus)
        a = jnp.exp(log_a)
        a_parts.append(a)
        u_parts.append(jnp.sqrt(-jnp.tanh(log_a) * (a * a + 1.0)) * (i * xcc))
    return a_parts, u_parts


def _rglru_rows(xr_ref, gr_ref, cw_ref, cb_ref, coeff_refs, aout_ref, buf, hcar):
    tc = xr_ref.shape[1]
    kw = cw_ref.shape[0]
    lo = SUBLANE - (kw - 1)
    buf[SUBLANE:SUBLANE + tc, :] = xr_ref[0]
    xc = cb_ref[...]
    for j in range(kw):
        xc = xc + buf[lo + j:lo + j + tc, :] * cw_ref[j:j + 1, :]
    buf[lo:SUBLANE, :] = buf[SUBLANE + tc - (kw - 1):SUBLANE + tc, :]
    a_parts, u_parts = _lru_coefficients(xc, *coeff_refs)
    for ch, (a, u) in enumerate(zip(a_parts, u_parts)):
        sl = slice(ch * LANE, (ch + 1) * LANE)
        hs = _linear_scan(a, u, hcar[:, sl])
        hcar[:, sl] = hs[tc - 1:tc, :]
        aout_ref[0, :, sl] = (gr_ref[0, :, sl] * hs).astype(aout_ref.dtype)


def _rglru_segments(xr_ref, gr_ref, cw_ref, cb_ref, coeff_refs, aout_ref, buf, hcar, xs, hs_s):
    tc, c = xr_ref.shape[1], xr_ref.shape[2]
    kw = cw_ref.shape[0]
    seg = tc // SUBLANE
    n_ch = c // LANE
    sub = lax.broadcasted_iota(jnp.int32, (SUBLANE, c), 0)

    for s in range(SUBLANE):
        for ch in range(n_ch):
            xs[ch, pl.ds(s, seg, stride=SUBLANE), :] = xr_ref[0, s * seg:(s + 1) * seg,
                                                             ch * LANE:(ch + 1) * LANE]
    x = jnp.concatenate([xs[ch] for ch in range(n_ch)], axis=1)

    wrapped = []
    for i in range(kw - 1):
        p = seg - (kw - 1) + i
        cur = x[p * SUBLANE:(p + 1) * SUBLANE, :]
        prev = buf[i * SUBLANE:(i + 1) * SUBLANE, :]
        wrapped.append(pltpu.roll(jnp.where(sub == SUBLANE - 1, prev, cur), 1, axis=0))
    buf[...] = x[(seg - (kw - 1)) * SUBLANE:, :]
    xc = cb_ref[...] + x * cw_ref[kw - 1:kw, :]
    for j in range(1, kw):
        shifted = jnp.concatenate(wrapped[kw - 1 - j:] + [x[:(seg - j) * SUBLANE, :]], axis=0)
        xc = xc + shifted * cw_ref[kw - 1 - j:kw - j, :]

    a_parts, u_parts = _lru_coefficients(xc, *coeff_refs)
    a = jnp.concatenate(a_parts, axis=1)
    u = jnp.concatenate(u_parts, axis=1)

    h = u[:SUBLANE, :]
    prod = a[:SUBLANE, :]
    local, prods = [h], [prod]
    for p in range(1, seg):
        a_p = a[p * SUBLANE:(p + 1) * SUBLANE, :]
        h = a_p * h + u[p * SUBLANE:(p + 1) * SUBLANE, :]
        prod = a_p * prod
        local.append(h)
        prods.append(prod)
    state = hcar[...]
    entering = []
    for s in range(SUBLANE):
        entering.append(state)
        state = h[s:s + 1, :] + prod[s:s + 1, :] * state
    hcar[...] = state
    h_in = jnp.concatenate(entering, axis=0)
    for p in range(seg):
        full = local[p] + prods[p] * h_in
        for ch in range(n_ch):
            hs_s[ch, p * SUBLANE:(p + 1) * SUBLANE, :] = full[:, ch * LANE:(ch + 1) * LANE]

    for ch in range(n_ch):
        sl = slice(ch * LANE, (ch + 1) * LANE)
        hn = jnp.concatenate([hs_s[ch, pl.ds(s, seg, stride=SUBLANE), :] for s in range(SUBLANE)], axis=0)
        aout_ref[0, :, sl] = (gr_ref[0, :, sl] * hn).astype(aout_ref.dtype)


def _rglru_kernel(xr_ref, gr_ref, conv0_ref, h0_ref, cw_ref, cb_ref, wg_ref, ba_ref, bx_ref, lam_ref,
                  aout_ref, convo_ref, hlast_ref, buf, hcar, *stage):
    tc = xr_ref.shape[1]
    kw = cw_ref.shape[0]
    segmented = bool(stage)

    @pl.when(pl.program_id(1) == 0)
    def _():
        hcar[...] = h0_ref[0]
        if segmented:
            buf[...] = jnp.zeros(buf.shape, F32)
            for i in range(kw - 1):
                buf[i * SUBLANE + SUBLANE - 1:(i + 1) * SUBLANE, :] = conv0_ref[0, i:i + 1, :]
        else:
            buf[SUBLANE - (kw - 1):SUBLANE, :] = conv0_ref[0]

    coeff_refs = (wg_ref, ba_ref, bx_ref, lam_ref)
    if segmented:
        _rglru_segments(xr_ref, gr_ref, cw_ref, cb_ref, coeff_refs, aout_ref, buf, hcar, *stage)
    else:
        _rglru_rows(xr_ref, gr_ref, cw_ref, cb_ref, coeff_refs, aout_ref, buf, hcar)
    convo_ref[0] = xr_ref[0, tc - (kw - 1):tc, :]
    hlast_ref[0] = hcar[...]


def _rglru(xr, gr, conv0, h0, conv_w, conv_b, w_gates, ba, bx, lam):
    b, t, c = xr.shape
    kw = conv_w.shape[0]
    tc = min(t, SCAN_ROWS)
    assert t % tc == 0 and tc % SUBLANE == 0 and tc >= kw - 1 and kw - 1 <= SUBLANE
    tok = pl.BlockSpec((1, tc, c), lambda i, j: (i, j, 0))
    per_b = lambda n: pl.BlockSpec((1, n, c), lambda i, j: (i, 0, 0))
    vec = _resident((1, c))
    if tc // SUBLANE >= SUBLANE:
        staging = pltpu.VMEM((c // LANE, tc, LANE), F32)
        scratch = [pltpu.VMEM(((kw - 1) * SUBLANE, c), F32), pltpu.VMEM((1, c), F32), staging, staging]
    else:
        scratch = [pltpu.VMEM((SUBLANE + tc, c), F32), pltpu.VMEM((1, c), F32)]
    return pl.pallas_call(
        _rglru_kernel,
        grid=(b, t // tc),
        in_specs=[tok, tok, per_b(kw - 1), per_b(1), _resident((kw, c)), vec,
                  _resident(w_gates.shape), vec, vec, vec],
        out_specs=[tok, per_b(kw - 1), per_b(1)],
        out_shape=[jax.ShapeDtypeStruct((b, t, c), BF16),
                   jax.ShapeDtypeStruct((b, kw - 1, c), F32),
                   jax.ShapeDtypeStruct((b, 1, c), F32)],
        scratch_shapes=scratch,
        compiler_params=_params("parallel", "arbitrary"),
        name="rglru",
    )(xr, gr, conv0, h0.reshape(b, 1, c), conv_w, conv_b.reshape(1, c), w_gates,
      ba.reshape(1, c), bx.reshape(1, c), lam.reshape(1, c))


def _cumsum_kernel(x_ref, o_ref):
    rows, t = x_ref.shape
    r = lax.broadcasted_iota(jnp.int32, (CUMSUM_CHUNK, CUMSUM_CHUNK), 0)
    c = lax.broadcasted_iota(jnp.int32, (CUMSUM_CHUNK, CUMSUM_CHUNK), 1)
    upper = (r <= c).astype(F32)
    carry = jnp.zeros((rows, 1), F32)
    for i in range(t // CUMSUM_CHUNK):
        sl = slice(i * CUMSUM_CHUNK, (i + 1) * CUMSUM_CHUNK)
        cs = jnp.dot(x_ref[:, sl], upper, preferred_element_type=F32,
                     precision=lax.Precision.HIGHEST) + carry
        o_ref[:, sl] = cs
        carry = cs[:, CUMSUM_CHUNK - 1:CUMSUM_CHUNK]


def _cumsum_lanes(x):
    b, h, t = x.shape
    n = b * h
    rows = min(n, CUMSUM_ROWS)
    assert t % CUMSUM_CHUNK == 0 and n % rows == 0
    spec = pl.BlockSpec((rows, t), lambda i: (i, 0))
    return pl.pallas_call(
        _cumsum_kernel, grid=(n // rows,), in_specs=[spec], out_specs=spec,
        out_shape=jax.ShapeDtypeStruct((n, t), F32),
        compiler_params=_params("parallel"), name="fgt_cumsum",
    )(x.reshape(n, t)).reshape(b, h, t)


def _head_column(f_blk, head):
    lane = lax.broadcasted_iota(jnp.int32, f_blk.shape, 1)
    return jnp.sum(jnp.where(lane == head, f_blk, 0.0), axis=-1, keepdims=True)


def _softmax_step(qm, kt, vt, bias, carry, mask=None):
    m_i, l_i, acc = carry
    s = lax.dot_general(qm, kt, (((1,), (1,)), ((), ())), preferred_element_type=F32) + bias
    if mask is not None:
        s = jnp.where(mask, s, -jnp.inf)
    m_new = jnp.maximum(m_i, jnp.max(s, axis=-1, keepdims=True))
    alpha = jnp.exp2(m_i - m_new)
    p = jnp.exp2(s - m_new)
    l_new = alpha * l_i + jnp.sum(p, axis=-1, keepdims=True)
    acc = alpha * acc + _dot(p.astype(BF16), vt)
    return m_new, l_new, acc


def _split3(x):
    hi = x.astype(BF16).astype(F32)
    rest = x - hi
    mid = rest.astype(BF16).astype(F32)
    lo = (rest - mid).astype(BF16).astype(F32)
    return hi, mid, lo


def _bias_lanes(f_rows):
    heads, t = f_rows.shape
    n = 3 * heads
    terms = _split3(f_rows)
    rows = 2 * SUBLANE
    assert 2 * n <= rows
    row = lax.broadcasted_iota(jnp.int32, (rows, t), 0)
    packed = jnp.zeros((rows, t), F32)
    for hh in range(heads):
        for j in range(3):
            term = terms[j][hh:hh + 1, :]
            packed = jnp.where((row == 3 * hh + j) | (row == n + 3 * hh + j), term, packed)
    cols = jnp.concatenate([packed, jnp.zeros((LANE - rows, t), F32)], axis=0).T
    lane = lax.broadcasted_iota(jnp.int32, (1, LANE), 1)
    key_lanes = jnp.where(lane < n, -cols, jnp.where(lane < 2 * n, 1.0, 0.0))
    query_lanes = jnp.where(lane < n, 1.0, jnp.where(lane < 2 * n, cols, 0.0))
    return key_lanes, query_lanes


def _qk_scores(kt, queries):
    return [lax.dot_general(kt, qa, (((1,), (1,)), ((), ())), preferred_element_type=F32)
            for qa in queries]


def _softmax_pv(scores, values, states, masks):
    partial = []
    for s, (m_i, l_i, acc), mask in zip(scores, states, masks):
        if mask is not None:
            s = jnp.where(mask, s, -jnp.inf)
        m_new = jnp.maximum(m_i, jnp.max(s, axis=0, keepdims=True))
        alpha = jnp.exp2(m_i - m_new)
        p = jnp.exp2(s - m_new)
        l_new = alpha * l_i + jnp.sum(p, axis=0, keepdims=True)
        partial.append((m_new, l_new, alpha, p.astype(BF16)))
    return [(m_new, l_new, alpha * acc + _dot(vt, p))
            for (m_new, l_new, alpha, p), (_, _, acc), vt in zip(partial, states, values)]


def _merge_softmax(a, b):
    (m_a, l_a, acc_a), (m_b, l_b, acc_b) = a, b
    m = jnp.maximum(m_a, m_b)
    w_a = jnp.exp2(m_a - m)
    w_b = jnp.exp2(m_b - m)
    return m, w_a * l_a + w_b * l_b, w_a * acc_a + w_b * acc_b


def _attn_prompt_kernel(q_ref, k_ref, v_ref, f_ref, o_ref, kaug, qext, vt_s, *, head_dim):
    t = q_ref.shape[1]
    tq = ATT_TILE
    heads = LANE // head_dim
    lane = lax.broadcasted_iota(jnp.int32, (1, LANE), 1)
    sub = lax.broadcasted_iota(jnp.int32, (LANE, 1), 0)
    key_pos = lax.broadcasted_iota(jnp.int32, (tq, tq), 0)
    qry_pos = lax.broadcasted_iota(jnp.int32, (tq, tq), 1)
    causal = key_pos <= qry_pos

    key_lanes, query_lanes = _bias_lanes(f_ref[0, 0] * LOG2_E)
    kaug[:, :LANE] = k_ref[0].astype(BF16)
    kaug[:, LANE:] = key_lanes.astype(BF16)
    qext[...] = query_lanes.astype(BF16)
    vt_s[...] = v_ref[0].T.astype(BF16)

    def query_tile(q0):
        q = q_ref[0, pl.ds(q0, tq), :]
        extra = qext[pl.ds(q0, tq), :]
        out = []
        for hh in range(heads):
            in_head = (lane >= hh * head_dim) & (lane < (hh + 1) * head_dim)
            lo = 3 * hh
            hi = 3 * heads + 3 * hh
            own = ((lane >= lo) & (lane < lo + 3)) | ((lane >= hi) & (lane < hi + 3))
            out.append(jnp.concatenate([jnp.where(in_head, q, jnp.zeros_like(q)),
                                        jnp.where(own, extra, jnp.zeros_like(extra))], axis=1))
        return out

    def init_state():
        return (jnp.full((1, tq), NEG_BIG, F32), jnp.zeros((1, tq), F32), jnp.zeros((LANE, tq), F32))

    def finish(q0, states):
        out_t = jnp.zeros((LANE, tq), F32)
        for hh, (_, l_i, acc) in enumerate(states):
            in_head = (sub >= hh * head_dim) & (sub < (hh + 1) * head_dim)
            out_t = jnp.where(in_head, acc / l_i, out_t)
        o_ref[0, pl.ds(q0, tq), :] = out_t.T.astype(o_ref.dtype)

    def keys(k0):
        return kaug[pl.ds(k0, tq), :]

    def values(k0):
        return vt_s[:, pl.ds(k0, tq)]

    n_chain = 2 * heads
    unmasked = [None] * n_chain

    def pair_queries(a):
        return query_tile(2 * a * tq) + query_tile((2 * a + 1) * tq)

    n_pairs = t // (2 * tq)
    queries = pair_queries(0)
    scores = _qk_scores(keys(0), queries)
    for a in range(n_pairs):
        states = [init_state() for _ in range(n_chain)]
        for j in range(2 * a):
            nxt = _qk_scores(keys((j + 1) * tq), queries)
            states = _softmax_pv(scores, [values(j * tq)] * n_chain, states, unmasked)
            scores = nxt
        q_lo, q_hi = 2 * a * tq, (2 * a + 1) * tq
        diagonal = scores + _qk_scores(keys(q_hi), queries[heads:])
        if a + 1 < n_pairs:
            queries = pair_queries(a + 1)
            scores = _qk_scores(keys(0), queries)
        states = _softmax_pv(
            diagonal,
            [values(q_lo)] * n_chain + [values(q_hi)] * heads,
            states + [init_state() for _ in range(heads)],
            [causal] * heads + [None] * heads + [causal] * heads)
        finish(q_lo, states[:heads])
        finish(q_hi, [_merge_softmax(states[heads + hh], states[n_chain + hh]) for hh in range(heads)])


def _attn_prompt(qs, k, v, f_row, head_dim):
    b, t, w = qs.shape
    h = f_row.shape[1]
    heads = LANE // head_dim
    assert LANE % head_dim == 0 and w % LANE == 0 and t % (2 * ATT_TILE) == 0 and h % heads == 0
    tok = pl.BlockSpec((1, t, LANE), lambda i, j: (i, 0, j))
    return pl.pallas_call(
        functools.partial(_attn_prompt_kernel, head_dim=head_dim),
        grid=(b, w // LANE),
        in_specs=[tok, tok, tok, pl.BlockSpec((1, 1, heads, t), lambda i, j: (i, j, 0, 0))],
        out_specs=tok,
        out_shape=jax.ShapeDtypeStruct((b, t, w), BF16),
        scratch_shapes=[pltpu.VMEM((t, 2 * LANE), BF16), pltpu.VMEM((t, LANE), BF16),
                        pltpu.VMEM((LANE, t), BF16)],
        compiler_params=_params("parallel", "arbitrary"),
        name="attn_prompt",
    )(qs, k, v, f_row.reshape(b, h // heads, heads, t))


def _attn_sample_kernel(q_ref, kn_ref, vn_ref, ck_ref, cv_ref, fn_ref, fnt_ref, fct_ref, o_ref,
                        kall, vall, *, head_dim):
    ts, past = q_ref.shape[1], ck_ref.shape[1]
    heads_per_block = LANE // head_dim
    pair = pl.program_id(1)
    kall[0:past, :] = ck_ref[0].astype(BF16)
    vall[0:past, :] = cv_ref[0].astype(BF16)
    kall[past:, :] = jnp.zeros((LANE, LANE), BF16)
    vall[past:, :] = jnp.zeros((LANE, LANE), BF16)
    kall[past:past + ts, :] = kn_ref[0].astype(BF16)
    vall[past:past + ts, :] = vn_ref[0].astype(BF16)
    lane = lax.broadcasted_iota(jnp.int32, (1, LANE), 1)
    row = lax.broadcasted_iota(jnp.int32, (ts, past + LANE), 0)
    col = lax.broadcasted_iota(jnp.int32, (ts, past + LANE), 1)
    visible = col <= past + row
    q = q_ref[0]
    out = jnp.zeros((ts, LANE), F32)
    for hh in range(heads_per_block):
        head = pair * heads_per_block + hh
        in_head = (lane >= hh * head_dim) & (lane < (hh + 1) * head_dim)
        qm = jnp.where(in_head, q, jnp.zeros_like(q))
        f_cache = fct_ref[0, pl.ds(head, 1), :]
        total = f_cache[:, past - 1:past]
        f_keys = jnp.concatenate([f_cache, total + fnt_ref[0, pl.ds(head, 1), :]], axis=1)
        bias = ((total + _head_column(fn_ref[0], head)) - f_keys) * LOG2_E
        init = (jnp.full((ts, 1), NEG_BIG, F32), jnp.zeros((ts, 1), F32), jnp.zeros((ts, LANE), F32))
        _, l_i, acc = _softmax_step(qm, kall[...], vall[...], bias, init, mask=visible)
        out = jnp.where(in_head, acc / l_i, out)
    o_ref[0] = out.astype(o_ref.dtype)


def _attn_sample(qs, k_new, v_new, cache_k, cache_v, fn_col, fn_row, fc_row, head_dim):
    b, ts, w = qs.shape
    past = cache_k.shape[1]
    h = fn_col.shape[-1]
    assert ts <= LANE and past % LANE == 0
    new = pl.BlockSpec((1, ts, LANE), lambda i, j: (i, 0, j))
    old = pl.BlockSpec((1, past, LANE), lambda i, j: (i, 0, j))
    return pl.pallas_call(
        functools.partial(_attn_sample_kernel, head_dim=head_dim),
        grid=(b, w // LANE),
        in_specs=[new, new, new, old, old,
                  pl.BlockSpec((1, ts, h), lambda i, j: (i, 0, 0)),
                  pl.BlockSpec((1, h, LANE), lambda i, j: (i, 0, 0)),
                  pl.BlockSpec((1, h, past), lambda i, j: (i, 0, 0))],
        out_specs=new,
        out_shape=jax.ShapeDtypeStruct((b, ts, w), BF16),
        scratch_shapes=[pltpu.VMEM((past + LANE, LANE), BF16), pltpu.VMEM((past + LANE, LANE), BF16)],
        compiler_params=_params("parallel", "arbitrary"),
        name="attn_sample",
    )(qs, k_new, v_new, cache_k, cache_v, fn_col, fn_row, fc_row)


def _gate_weights(wa, wx):
    nb, bw, _ = wa.shape
    per = LANE // bw

    def chunked(w):
        w = w.reshape(nb // per, per, bw, bw)
        eye = jnp.eye(per, dtype=w.dtype)
        return jnp.einsum("cpij,pq->cpiqj", w, eye).reshape(nb // per, LANE, LANE)

    return jnp.concatenate([chunked(wa), chunked(wx)], axis=-1).astype(BF16)


def _pad_lanes(x, n):
    return jnp.pad(x, [(0, 0)] * (x.ndim - 1) + [(0, n - x.shape[-1])])


def _mixer(x, mod, w, conv0, h0, attend):
    b, t, d = x.shape
    xr, gr, qs, k, v, logf, ga, gb = _inproj(
        x, mod, w["mix_norm"], w["w_main"], w["w_fgt"], w["w_gate"], w["b_fgt"],
        w["n_heads"], w["q_scale"])
    a_out, conv_state, h_last = _rglru(xr, gr, conv0, h0, w["conv_w"], w["conv_b"], w["w_gates"],
                                       w["rg_ba"], w["rg_bx"], w["rg_lam"])
    att = attend(qs, k, v, logf)
    merge = (a_out, att, ga, gb, w["w_br_rnn"], w["w_br_att"], w["w_out"])
    return merge, k, v, logf, conv_state, h_last.reshape(b, -1)


def kernel(x_prompt, x_sample, cache_k, cache_v, cache_logf, state_conv, state_rglru, c_prompt, c_sample, w_ada, b_ada, ffn1_norm, ffn1_w1, ffn1_w3, ffn1_w2, mix_norm, w_in, b_fgt, conv_w, conv_b, rg_wa, rg_ba, rg_wx, rg_bx, rg_lam, w_br_rnn, w_br_att, w_out, ffn2_norm, ffn2_w1, ffn2_w3, ffn2_w2, final_norm, w_ada_f, b_ada_f):
    bp, seq, d = x_prompt.shape
    bs, dec_seq, _ = x_sample.shape
    depth = w_ada.shape[0]
    n_heads = b_fgt.shape[-1]
    head_dim = cache_k.shape[-1]
    d_rnn = conv_w.shape[-1]
    kw = conv_w.shape[1]
    att_w = n_heads * head_dim
    past = cache_k.shape[2]
    assert d_rnn == d and att_w == d and n_heads <= LANE
    assert w_in.shape[-1] == 2 * d_rnn + 3 * att_w + n_heads + 2 * d

    c_all = jnp.concatenate([c_prompt, c_sample], axis=0)
    mod_f = _ada_mod(c_all, w_ada_f, b_ada_f).reshape(bp + bs, 2, d)
    xp, xs = x_prompt, x_sample
    outs_p, outs_s = [], []
    for l in range(depth):
        mod = _ada_mod(c_all, w_ada[l], b_ada[l]).reshape(bp + bs, N_MOD, d)
        mp, ms = mod[:bp], mod[bp:]
        n_main = 2 * d_rnn + 3 * att_w
        w = {
            "mix_norm": mix_norm[l], "n_heads": n_heads, "q_scale": head_dim ** -0.5 * LOG2_E,
            "w_main": w_in[l, :, :n_main].astype(BF16),
            "w_fgt": _pad_lanes(w_in[l, :, n_main:n_main + n_heads], LANE).astype(BF16),
            "w_gate": w_in[l, :, n_main + n_heads:].astype(BF16),
            "b_fgt": _pad_lanes(b_fgt[l].reshape(1, n_heads), LANE),
            "conv_w": conv_w[l], "conv_b": conv_b[l], "w_gates": _gate_weights(rg_wa[l], rg_wx[l]),
            "rg_ba": rg_ba[l], "rg_bx": rg_bx[l], "rg_lam": rg_lam[l],
            "w_br_rnn": w_br_rnn[l].astype(BF16), "w_br_att": w_br_att[l].astype(BF16),
            "w_out": w_out[l].astype(BF16),
        }
        f1 = (ffn1_norm[l], ffn1_w1[l].astype(BF16), ffn1_w3[l].astype(BF16), ffn1_w2[l].astype(BF16))
        f2 = (ffn2_norm[l], ffn2_w1[l].astype(BF16), ffn2_w3[l].astype(BF16), ffn2_w2[l].astype(BF16))
        last = l == depth - 1

        xp = _ffn(xp, mp, *f1, base=0)
        xs = _ffn(xs, ms, *f1, base=0)

        def attend_prompt(qs, k, v, logf):
            return _attn_prompt(qs, k, v, _cumsum_lanes(jnp.swapaxes(logf, 1, 2)), head_dim)

        def attend_sample(qs, k, v, logf):
            fc_row = _cumsum_lanes(jnp.swapaxes(cache_logf[l], 1, 2))
            fn_row = _cumsum_lanes(_pad_lanes(jnp.swapaxes(logf, 1, 2), LANE))
            fn_col = jnp.swapaxes(fn_row[:, :, :dec_seq], 1, 2)
            return _attn_sample(qs, k, v, cache_k[l].reshape(bs, past, att_w),
                                cache_v[l].reshape(bs, past, att_w), fn_col, fn_row, fc_row, head_dim)

        merge_p, kp, vp, lp, cp, hp = _mixer(xp, mp, w, jnp.zeros((bp, kw - 1, d_rnn), F32),
                                             jnp.zeros((bp, d_rnn), F32), attend_prompt)
        merge_s, ks, vs, ls, cs, hs = _mixer(xs, ms, w, state_conv[l], state_rglru[l], attend_sample)

        xp = _ffn(xp, mp, *f2, base=6, merge=merge_p,
                  final=(final_norm, mod_f[:bp]) if last else None)
        xs = _ffn(xs, ms, *f2, base=6, merge=merge_s,
                  final=(final_norm, mod_f[bp:]) if last else None)
        heads_p = (bp, seq, n_heads, head_dim)
        heads_s = (bs, dec_seq, n_heads, head_dim)
        outs_p.append((kp.reshape(heads_p), vp.reshape(heads_p), lp, cp, hp))
        outs_s.append((ks.reshape(heads_s), vs.reshape(heads_s), ls, cs, hs))
    stack = lambda outs: tuple(jnp.stack(leaf) for leaf in zip(*outs))
    return (xp, xs) + stack(outs_p) + stack(outs_s)
```

```python
import functools
import math

import jax
import jax.numpy as jnp
from jax import lax
from jax.experimental import pallas as pl
from jax.experimental.pallas import tpu as pltpu

F32 = jnp.float32
BF16 = jnp.bfloat16

LANE = 128
SUBLANE = 8
VMEM_BYTES_V7X = 64 * 1024 * 1024
VMEM_LIMIT = VMEM_BYTES_V7X - 8 * 1024 * 1024

EPS = 1e-6
MACARON_W = 0.5
RG_C = 8.0
N_MOD = 9

ROWS_PER_STEP = 512
SCAN_ROWS = 256
ATT_TILE = 256
CUMSUM_CHUNK = 128
CUMSUM_ROWS = 128
NEG_BIG = -1e30
LOG2_E = math.log2(math.e)


def _params(*sem):
    return pltpu.CompilerParams(dimension_semantics=sem, vmem_limit_bytes=VMEM_LIMIT)


def _resident(shape):
    nd = len(shape)
    return pl.BlockSpec(shape, lambda *_: (0,) * nd, pipeline_mode=pl.Buffered(1))


def _row_blocks(b, t, rows):
    if t >= rows:
        assert t % rows == 0, (t, rows)
        return 1, rows
    bb = max(1, min(b, rows // t))
    while b % bb:
        bb -= 1
    return bb, t


def _mod_norm(x, g, shift, scale):
    inv = lax.rsqrt(jnp.mean(x * x, axis=-1, keepdims=True) + EPS)
    return (x * inv) * g * (1.0 + scale) + shift


def _dot(a, b):
    return jnp.dot(a, b, preferred_element_type=F32)


def _log_sigmoid(x):
    return jnp.minimum(x, 0.0) - jnp.log1p(jnp.exp(-jnp.abs(x)))


def _ada_kernel(c_ref, w_ref, b_ref, o_ref):
    c = c_ref[...]
    s = (c * jax.nn.sigmoid(c)).astype(BF16)
    o_ref[...] = _dot(s, w_ref[...].astype(BF16)) + b_ref[...]


def _ada_mod(c, w, b):
    bc, d = c.shape
    n = w.shape[1]
    tn = 1024
    assert n % tn == 0
    return pl.pallas_call(
        _ada_kernel,
        grid=(n // tn,),
        in_specs=[
            pl.BlockSpec((bc, d), lambda j: (0, 0)),
            pl.BlockSpec((d, tn), lambda j: (0, j)),
            pl.BlockSpec((1, tn), lambda j: (0, j)),
        ],
        out_specs=pl.BlockSpec((bc, tn), lambda j: (0, j)),
        out_shape=jax.ShapeDtypeStruct((bc, n), F32),
        compiler_params=_params("parallel"),
        name="ada_mod",
    )(c, w, b.reshape(1, n))


def _ffn_kernel(x_ref, mod_ref, *rest, base, merge, final):
    rest = list(rest)
    bb, tm, d = x_ref.shape
    n_part = 2 if tm % (2 * SUBLANE) == 0 else 1
    tp = tm // n_part
    rows = bb * tp

    def parts(ref):
        return [ref[:, i * tp:(i + 1) * tp, :] for i in range(n_part)]

    xs = parts(x_ref)
    if merge:
        a_ref, t_ref, ga_ref, gb_ref, wr_ref, wa_ref, wo_ref = rest[:7]
        rest = rest[7:]
        prs = [_dot(a.reshape(rows, -1), wr_ref[...]) for a in parts(a_ref)]
        pas = [_dot(t.reshape(rows, -1), wa_ref[...]) for t in parts(t_ref)]
        zs = [(jax.nn.sigmoid(ga.reshape(rows, d)) * pr
               + jax.nn.sigmoid(gb.reshape(rows, d)) * pa).astype(BF16)
              for ga, gb, pr, pa in zip(parts(ga_ref), parts(gb_ref), prs, pas)]
        out_gate = 1.0 + mod_ref[:, 5:6, :]
        xs = [x + out_gate * _dot(z, wo_ref[...]).reshape(bb, tp, d) for x, z in zip(xs, zs)]
    g_ref, w1_ref, w3_ref, w2_ref = rest[:4]
    rest = rest[4:]
    if final:
        fg_ref, fmod_ref, o_ref = rest
    else:
        (o_ref,) = rest
    shift = mod_ref[:, base:base + 1, :]
    scale = mod_ref[:, base + 1:base + 2, :]
    gate = 1.0 + mod_ref[:, base + 2:base + 3, :]
    hbs = [_mod_norm(x, g_ref[...], shift, scale).reshape(rows, d).astype(BF16) for x in xs]
    ups = [(_dot(hb, w1_ref[...]), _dot(hb, w3_ref[...])) for hb in hbs]
    ts = [((a * jax.nn.sigmoid(a)) * b).astype(BF16) for a, b in ups]
    ys = [_dot(t, w2_ref[...]).reshape(bb, tp, d) for t in ts]
    for i, (x, y) in enumerate(zip(xs, ys)):
        out = x + MACARON_W * gate * y
        if final:
            out = _mod_norm(out, fg_ref[...], fmod_ref[:, 0:1, :], fmod_ref[:, 1:2, :])
        o_ref[:, i * tp:(i + 1) * tp, :] = out


def _ffn(x, mod, g, w1, w3, w2, base, merge=None, final=None):
    b, t, d = x.shape
    dff = w1.shape[1]
    bb, tm = _row_blocks(b, t, ROWS_PER_STEP)
    tok = lambda n: pl.BlockSpec((bb, tm, n), lambda i, j: (i, j, 0))
    modspec = lambda n: pl.BlockSpec((bb, n, d), lambda i, j: (i, 0, 0))
    in_specs = [tok(d), modspec(mod.shape[1])]
    args = [x, mod]
    if merge is not None:
        a_out, att, ga, gb, w_br_rnn, w_br_att, w_out = merge
        in_specs += [tok(a_out.shape[-1]), tok(att.shape[-1]), tok(d), tok(d),
                     _resident(w_br_rnn.shape), _resident(w_br_att.shape), _resident(w_out.shape)]
        args += list(merge)
    in_specs += [_resident((1, d)), _resident((d, dff)), _resident((d, dff)), _resident((dff, d))]
    args += [g.reshape(1, d), w1, w3, w2]
    if final is not None:
        fg, fmod = final
        in_specs += [_resident((1, d)), modspec(fmod.shape[1])]
        args += [fg.reshape(1, d), fmod]
    return pl.pallas_call(
        functools.partial(_ffn_kernel, base=base, merge=merge is not None, final=final is not None),
        grid=(b // bb, t // tm),
        in_specs=in_specs,
        out_specs=tok(d),
        out_shape=jax.ShapeDtypeStruct((b, t, d), F32),
        compiler_params=_params("parallel", "parallel"),
        name="ffn" + ("_merge" if merge is not None else "") + ("_final" if final is not None else ""),
    )(*args)


def _inproj_kernel(x_ref, mod_ref, g_ref, wm_ref, wf_ref, wg_ref, bf_ref,
                   xr_o, gr_o, q_o, k_o, v_o, lf_o, ga_o, gb_o, *, q_scale):
    x = x_ref[...]
    bb, tm, d = x.shape
    hb = _mod_norm(x, g_ref[...], mod_ref[:, 3:4, :], mod_ref[:, 4:5, :])
    hb = hb.reshape(bb * tm, d).astype(BF16)

    def proj(w_ref, idx):
        return _dot(hb, w_ref[:, idx * d:(idx + 1) * d]).reshape(bb, tm, d)

    xr_o[...] = proj(wm_ref, 0)
    gr_o[...] = jax.nn.gelu(proj(wm_ref, 1))
    q_o[...] = (proj(wm_ref, 2) * q_scale).astype(q_o.dtype)
    k_o[...] = proj(wm_ref, 3)
    v_o[...] = proj(wm_ref, 4)
    ga_o[...] = proj(wg_ref, 0)
    gb_o[...] = proj(wg_ref, 1)
    n_heads = lf_o.shape[-1]
    lf = _log_sigmoid(_dot(hb, wf_ref[...]) + bf_ref[...])
    lf_o[...] = lf[:, :n_heads].reshape(bb, tm, n_heads)


def _inproj(x, mod, g, w_main, w_fgt, w_gate, b_fgt_pad, n_heads, q_scale):
    b, t, d = x.shape
    bb, tm = _row_blocks(b, t, ROWS_PER_STEP)
    tok = pl.BlockSpec((bb, tm, d), lambda i, j: (i, j, 0))
    f32_out = jax.ShapeDtypeStruct((b, t, d), F32)
    return pl.pallas_call(
        functools.partial(_inproj_kernel, q_scale=q_scale),
        grid=(b // bb, t // tm),
        in_specs=[tok, pl.BlockSpec((bb, mod.shape[1], d), lambda i, j: (i, 0, 0)),
                  _resident((1, d)), _resident(w_main.shape), _resident(w_fgt.shape),
                  _resident(w_gate.shape), _resident((1, LANE))],
        out_specs=[tok, tok, tok, tok, tok,
                   pl.BlockSpec((bb, tm, n_heads), lambda i, j: (i, j, 0)), tok, tok],
        out_shape=[f32_out, f32_out, jax.ShapeDtypeStruct((b, t, d), BF16), f32_out, f32_out,
                   jax.ShapeDtypeStruct((b, t, n_heads), F32), f32_out, f32_out],
        compiler_params=_params("parallel", "parallel"),
        name="in_proj",
    )(x, mod, g.reshape(1, d), w_main, w_fgt, w_gate, b_fgt_pad)


def _linear_scan(a, u, h_in):
    t = a.shape[0]
    n = t // SUBLANE
    a3 = a.reshape(n, SUBLANE, LANE)
    u3 = u.reshape(n, SUBLANE, LANE)
    row = lax.broadcasted_iota(jnp.int32, (n, SUBLANE, LANE), 1)
    for dist in (1, 2, 4):
        keep = row >= dist
        u_prev = pltpu.roll(u3, dist, axis=1)
        a_prev = pltpu.roll(a3, dist, axis=1)
        u3 = jnp.where(keep, a3 * u_prev, 0.0) + u3
        a3 = jnp.where(keep, a3 * a_prev, a3)
    h = h_in
    outs = []
    for grp in range(n):
        hg = u3[grp] + a3[grp] * h
        outs.append(hg)
        h = hg[SUBLANE - 1:SUBLANE, :]
    return jnp.concatenate(outs, axis=0)


def _lru_coefficients(xc, wg_ref, ba_ref, bx_ref, lam_ref):
    a_parts, u_parts = [], []
    for ch in range(xc.shape[1] // LANE):
        sl = slice(ch * LANE, (ch + 1) * LANE)
        xcc = xc[:, sl]
        gates = _dot(xcc.astype(BF16), wg_ref[ch])
        r = 0.5 * jnp.tanh(0.5 * (gates[:, :LANE] + ba_ref[:, sl])) + 0.5
        i = 0.5 * jnp.tanh(0.5 * (gates[:, LANE:] + bx_ref[:, sl])) + 0.5
        neg_lam = -lam_ref[:, sl]
        softplus = jnp.maximum(neg_lam, 0.0) + jnp.log1p(jnp.exp(-jnp.abs(neg_lam)))
        log_a = r * ((-RG_C) * softplus)
        a = jnp.exp(log_a)
        a_parts.append(a)
        u_parts.append(jnp.sqrt(-jnp.tanh(log_a) * (a * a + 1.0)) * (i * xcc))
    return a_parts, u_parts


def _rglru_rows(xr_ref, gr_ref, cw_ref, cb_ref, coeff_refs, aout_ref, buf, hcar):
    tc = xr_ref.shape[1]
    kw = cw_ref.shape[0]
    lo = SUBLANE - (kw - 1)
    buf[SUBLANE:SUBLANE + tc, :] = xr_ref[0]
    xc = cb_ref[...]
    for j in range(kw):
        xc = xc + buf[lo + j:lo + j + tc, :] * cw_ref[j:j + 1, :]
    buf[lo:SUBLANE, :] = buf[SUBLANE + tc - (kw - 1):SUBLANE + tc, :]
    a_parts, u_parts = _lru_coefficients(xc, *coeff_refs)
    for ch, (a, u) in enumerate(zip(a_parts, u_parts)):
        sl = slice(ch * LANE, (ch + 1) * LANE)
        hs = _linear_scan(a, u, hcar[:, sl])
        hcar[:, sl] = hs[tc - 1:tc, :]
        aout_ref[0, :, sl] = (gr_ref[0, :, sl] * hs).astype(aout_ref.dtype)


def _rglru_segments(xr_ref, gr_ref, cw_ref, cb_ref, coeff_refs, aout_ref, buf, hcar, xs, hs_s):
    tc, c = xr_ref.shape[1], xr_ref.shape[2]
    kw = cw_ref.shape[0]
    seg = tc // SUBLANE
    n_ch = c // LANE
    sub = lax.broadcasted_iota(jnp.int32, (SUBLANE, c), 0)

    for s in range(SUBLANE):
        for ch in range(n_ch):
            xs[ch, pl.ds(s, seg, stride=SUBLANE), :] = xr_ref[0, s * seg:(s + 1) * seg,
                                                             ch * LANE:(ch + 1) * LANE]
    x = jnp.concatenate([xs[ch] for ch in range(n_ch)], axis=1)

    wrapped = []
    for i in range(kw - 1):
        p = seg - (kw - 1) + i
        cur = x[p * SUBLANE:(p + 1) * SUBLANE, :]
        prev = buf[i * SUBLANE:(i + 1) * SUBLANE, :]
        wrapped.append(pltpu.roll(jnp.where(sub == SUBLANE - 1, prev, cur), 1, axis=0))
    buf[...] = x[(seg - (kw - 1)) * SUBLANE:, :]
    xc = cb_ref[...] + x * cw_ref[kw - 1:kw, :]
    for j in range(1, kw):
        shifted = jnp.concatenate(wrapped[kw - 1 - j:] + [x[:(seg - j) * SUBLANE, :]], axis=0)
        xc = xc + shifted * cw_ref[kw - 1 - j:kw - j, :]

    a_parts, u_parts = _lru_coefficients(xc, *coeff_refs)
    a = jnp.concatenate(a_parts, axis=1)
    u = jnp.concatenate(u_parts, axis=1)

    h = u[:SUBLANE, :]
    prod = a[:SUBLANE, :]
    local, prods = [h], [prod]
    for p in range(1, seg):
        a_p = a[p * SUBLANE:(p + 1) * SUBLANE, :]
        h = a_p * h + u[p * SUBLANE:(p + 1) * SUBLANE, :]
        prod = a_p * prod
        local.append(h)
        prods.append(prod)
    state = hcar[...]
    entering = []
    for s in range(SUBLANE):
        entering.append(state)
        state = h[s:s + 1, :] + prod[s:s + 1, :] * state
    hcar[...] = state
    h_in = jnp.concatenate(entering, axis=0)
    for p in range(seg):
        full = local[p] + prods[p] * h_in
        for ch in range(n_ch):
            hs_s[ch, p * SUBLANE:(p + 1) * SUBLANE, :] = full[:, ch * LANE:(ch + 1) * LANE]

    for ch in range(n_ch):
        sl = slice(ch * LANE, (ch + 1) * LANE)
        hn = jnp.concatenate([hs_s[ch, pl.ds(s, seg, stride=SUBLANE), :] for s in range(SUBLANE)], axis=0)
        aout_ref[0, :, sl] = (gr_ref[0, :, sl] * hn).astype(aout_ref.dtype)


def _rglru_kernel(xr_ref, gr_ref, conv0_ref, h0_ref, cw_ref, cb_ref, wg_ref, ba_ref, bx_ref, lam_ref,
                  aout_ref, convo_ref, hlast_ref, buf, hcar, *stage):
    tc = xr_ref.shape[1]
    kw = cw_ref.shape[0]
    segmented = bool(stage)

    @pl.when(pl.program_id(1) == 0)
    def _():
        hcar[...] = h0_ref[0]
        if segmented:
            buf[...] = jnp.zeros(buf.shape, F32)
            for i in range(kw - 1):
                buf[i * SUBLANE + SUBLANE - 1:(i + 1) * SUBLANE, :] = conv0_ref[0, i:i + 1, :]
        else:
            buf[SUBLANE - (kw - 1):SUBLANE, :] = conv0_ref[0]

    coeff_refs = (wg_ref, ba_ref, bx_ref, lam_ref)
    if segmented:
        _rglru_segments(xr_ref, gr_ref, cw_ref, cb_ref, coeff_refs, aout_ref, buf, hcar, *stage)
    else:
        _rglru_rows(xr_ref, gr_ref, cw_ref, cb_ref, coeff_refs, aout_ref, buf, hcar)
    convo_ref[0] = xr_ref[0, tc - (kw - 1):tc, :]
    hlast_ref[0] = hcar[...]


def _rglru(xr, gr, conv0, h0, conv_w, conv_b, w_gates, ba, bx, lam):
    b, t, c = xr.shape
    kw = conv_w.shape[0]
    tc = min(t, SCAN_ROWS)
    assert t % tc == 0 and tc % SUBLANE == 0 and tc >= kw - 1 and kw - 1 <= SUBLANE
    tok = pl.BlockSpec((1, tc, c), lambda i, j: (i, j, 0))
    per_b = lambda n: pl.BlockSpec((1, n, c), lambda i, j: (i, 0, 0))
    vec = _resident((1, c))
    if tc // SUBLANE >= SUBLANE:
        staging = pltpu.VMEM((c // LANE, tc, LANE), F32)
        scratch = [pltpu.VMEM(((kw - 1) * SUBLANE, c), F32), pltpu.VMEM((1, c), F32), staging, staging]
    else:
        scratch = [pltpu.VMEM((SUBLANE + tc, c), F32), pltpu.VMEM((1, c), F32)]
    return pl.pallas_call(
        _rglru_kernel,
        grid=(b, t // tc),
        in_specs=[tok, tok, per_b(kw - 1), per_b(1), _resident((kw, c)), vec,
                  _resident(w_gates.shape), vec, vec, vec],
        out_specs=[tok, per_b(kw - 1), per_b(1)],
        out_shape=[jax.ShapeDtypeStruct((b, t, c), BF16),
                   jax.ShapeDtypeStruct((b, kw - 1, c), F32),
                   jax.ShapeDtypeStruct((b, 1, c), F32)],
        scratch_shapes=scratch,
        compiler_params=_params("parallel", "arbitrary"),
        name="rglru",
    )(xr, gr, conv0, h0.reshape(b, 1, c), conv_w, conv_b.reshape(1, c), w_gates,
      ba.reshape(1, c), bx.reshape(1, c), lam.reshape(1, c))


def _cumsum_kernel(x_ref, o_ref):
    rows, t = x_ref.shape
    r = lax.broadcasted_iota(jnp.int32, (CUMSUM_CHUNK, CUMSUM_CHUNK), 0)
    c = lax.broadcasted_iota(jnp.int32, (CUMSUM_CHUNK, CUMSUM_CHUNK), 1)
    upper = (r <= c).astype(F32)
    carry = jnp.zeros((rows, 1), F32)
    for i in range(t // CUMSUM_CHUNK):
        sl = slice(i * CUMSUM_CHUNK, (i + 1) * CUMSUM_CHUNK)
        cs = jnp.dot(x_ref[:, sl], upper, preferred_element_type=F32,
                     precision=lax.Precision.HIGHEST) + carry
        o_ref[:, sl] = cs
        carry = cs[:, CUMSUM_CHUNK - 1:CUMSUM_CHUNK]


def _cumsum_lanes(x):
    b, h, t = x.shape
    n = b * h
    rows = min(n, CUMSUM_ROWS)
    assert t % CUMSUM_CHUNK == 0 and n % rows == 0
    spec = pl.BlockSpec((rows, t), lambda i: (i, 0))
    return pl.pallas_call(
        _cumsum_kernel, grid=(n // rows,), in_specs=[spec], out_specs=spec,
        out_shape=jax.ShapeDtypeStruct((n, t), F32),
        compiler_params=_params("parallel"), name="fgt_cumsum",
    )(x.reshape(n, t)).reshape(b, h, t)


def _head_column(f_blk, head):
    lane = lax.broadcasted_iota(jnp.int32, f_blk.shape, 1)
    return jnp.sum(jnp.where(lane == head, f_blk, 0.0), axis=-1, keepdims=True)


def _softmax_step(qm, kt, vt, bias, carry, mask=None):
    m_i, l_i, acc = carry
    s = lax.dot_general(qm, kt, (((1,), (1,)), ((), ())), preferred_element_type=F32) + bias
    if mask is not None:
        s = jnp.where(mask, s, -jnp.inf)
    m_new = jnp.maximum(m_i, jnp.max(s, axis=-1, keepdims=True))
    alpha = jnp.exp2(m_i - m_new)
    p = jnp.exp2(s - m_new)
    l_new = alpha * l_i + jnp.sum(p, axis=-1, keepdims=True)
    acc = alpha * acc + _dot(p.astype(BF16), vt)
    return m_new, l_new, acc


def _split3(x):
    hi = x.astype(BF16).astype(F32)
    rest = x - hi
    mid = rest.astype(BF16).astype(F32)
    lo = (rest - mid).astype(BF16).astype(F32)
    return hi, mid, lo


def _bias_lanes(f_rows):
    heads, t = f_rows.shape
    n = 3 * heads
    terms = _split3(f_rows)
    rows = 2 * SUBLANE
    assert 2 * n <= rows
    row = lax.broadcasted_iota(jnp.int32, (rows, t), 0)
    packed = jnp.zeros((rows, t), F32)
    for hh in range(heads):
        for j in range(3):
            term = terms[j][hh:hh + 1, :]
            packed = jnp.where((row == 3 * hh + j) | (row == n + 3 * hh + j), term, packed)
    cols = jnp.concatenate([packed, jnp.zeros((LANE - rows, t), F32)], axis=0).T
    lane = lax.broadcasted_iota(jnp.int32, (1, LANE), 1)
    key_lanes = jnp.where(lane < n, -cols, jnp.where(lane < 2 * n, 1.0, 0.0))
    query_lanes = jnp.where(lane < n, 1.0, jnp.where(lane < 2 * n, cols, 0.0))
    return key_lanes, query_lanes


def _qk_scores(kt, queries):
    return [lax.dot_general(kt, qa, (((1,), (1,)), ((), ())), preferred_element_type=F32)
            for qa in queries]


def _softmax_pv(scores, values, states, masks):
    partial = []
    for s, (m_i, l_i, acc), mask in zip(scores, states, masks):
        if mask is not None:
            s = jnp.where(mask, s, -jnp.inf)
        m_new = jnp.maximum(m_i, jnp.max(s, axis=0, keepdims=True))
        alpha = jnp.exp2(m_i - m_new)
        p = jnp.exp2(s - m_new)
        l_new = alpha * l_i + jnp.sum(p, axis=0, keepdims=True)
        partial.append((m_new, l_new, alpha, p.astype(BF16)))
    return [(m_new, l_new, alpha * acc + _dot(vt, p))
            for (m_new, l_new, alpha, p), (_, _, acc), vt in zip(partial, states, values)]


def _merge_softmax(a, b):
    (m_a, l_a, acc_a), (m_b, l_b, acc_b) = a, b
    m = jnp.maximum(m_a, m_b)
    w_a = jnp.exp2(m_a - m)
    w_b = jnp.exp2(m_b - m)
    return m, w_a * l_a + w_b * l_b, w_a * acc_a + w_b * acc_b


def _attn_prompt_kernel(q_ref, k_ref, v_ref, f_ref, o_ref, kaug, qext, vt_s, *, head_dim):
    t = q_ref.shape[1]
    tq = ATT_TILE
    heads = LANE // head_dim
    lane = lax.broadcasted_iota(jnp.int32, (1, LANE), 1)
    sub = lax.broadcasted_iota(jnp.int32, (LANE, 1), 0)
    key_pos = lax.broadcasted_iota(jnp.int32, (tq, tq), 0)
    qry_pos = lax.broadcasted_iota(jnp.int32, (tq, tq), 1)
    causal = key_pos <= qry_pos

    key_lanes, query_lanes = _bias_lanes(f_ref[0, 0] * LOG2_E)
    kaug[:, :LANE] = k_ref[0].astype(BF16)
    kaug[:, LANE:] = key_lanes.astype(BF16)
    qext[...] = query_lanes.astype(BF16)
    vt_s[...] = v_ref[0].T.astype(BF16)

    def query_tile(q0):
        q = q_ref[0, pl.ds(q0, tq), :]
        extra = qext[pl.ds(q0, tq), :]
        out = []
        for hh in range(heads):
            in_head = (lane >= hh * head_dim) & (lane < (hh + 1) * head_dim)
            lo = 3 * hh
            hi = 3 * heads + 3 * hh
            own = ((lane >= lo) & (lane < lo + 3)) | ((lane >= hi) & (lane < hi + 3))
            out.append(jnp.concatenate([jnp.where(in_head, q, jnp.zeros_like(q)),
                                        jnp.where(own, extra, jnp.zeros_like(extra))], axis=1))
        return out

    def init_state():
        return (jnp.full((1, tq), NEG_BIG, F32), jnp.zeros((1, tq), F32), jnp.zeros((LANE, tq), F32))

    def finish(q0, states):
        out_t = jnp.zeros((LANE, tq), F32)
        for hh, (_, l_i, acc) in enumerate(states):
            in_head = (sub >= hh * head_dim) & (sub < (hh + 1) * head_dim)
            out_t = jnp.where(in_head, acc / l_i, out_t)
        o_ref[0, pl.ds(q0, tq), :] = out_t.T.astype(o_ref.dtype)

    def keys(k0):
        return kaug[pl.ds(k0, tq), :]

    def values(k0):
        return vt_s[:, pl.ds(k0, tq)]

    n_chain = 2 * heads
    unmasked = [None] * n_chain

    def pair_queries(a):
        return query_tile(2 * a * tq) + query_tile((2 * a + 1) * tq)

    n_pairs = t // (2 * tq)
    queries = pair_queries(0)
    scores = _qk_scores(keys(0), queries)
    for a in range(n_pairs):
        states = [init_state() for _ in range(n_chain)]
        for j in range(2 * a):
            nxt = _qk_scores(keys((j + 1) * tq), queries)
            states = _softmax_pv(scores, [values(j * tq)] * n_chain, states, unmasked)
            scores = nxt
        q_lo, q_hi = 2 * a * tq, (2 * a + 1) * tq
        diagonal = scores + _qk_scores(keys(q_hi), queries[heads:])
        if a + 1 < n_pairs:
            queries = pair_queries(a + 1)
            scores = _qk_scores(keys(0), queries)
        states = _softmax_pv(
            diagonal,
            [values(q_lo)] * n_chain + [values(q_hi)] * heads,
            states + [init_state() for _ in range(heads)],
            [causal] * heads + [None] * heads + [causal] * heads)
        finish(q_lo, states[:heads])
        finish(q_hi, [_merge_softmax(states[heads + hh], states[n_chain + hh]) for hh in range(heads)])


def _attn_prompt(qs, k, v, f_row, head_dim):
    b, t, w = qs.shape
    h = f_row.shape[1]
    heads = LANE // head_dim
    assert LANE % head_dim == 0 and w % LANE == 0 and t % (2 * ATT_TILE) == 0 and h % heads == 0
    tok = pl.BlockSpec((1, t, LANE), lambda i, j: (i, 0, j))
    return pl.pallas_call(
        functools.partial(_attn_prompt_kernel, head_dim=head_dim),
        grid=(b, w // LANE),
        in_specs=[tok, tok, tok, pl.BlockSpec((1, 1, heads, t), lambda i, j: (i, j, 0, 0))],
        out_specs=tok,
        out_shape=jax.ShapeDtypeStruct((b, t, w), BF16),
        scratch_shapes=[pltpu.VMEM((t, 2 * LANE), BF16), pltpu.VMEM((t, LANE), BF16),
                        pltpu.VMEM((LANE, t), BF16)],
        compiler_params=_params("parallel", "arbitrary"),
        name="attn_prompt",
    )(qs, k, v, f_row.reshape(b, h // heads, heads, t))


def _attn_sample_kernel(q_ref, kn_ref, vn_ref, ck_ref, cv_ref, fn_ref, fnt_ref, fct_ref, o_ref,
                        kall, vall, *, head_dim):
    ts, past = q_ref.shape[1], ck_ref.shape[1]
    heads_per_block = LANE // head_dim
    pair = pl.program_id(1)
    kall[0:past, :] = ck_ref[0].astype(BF16)
    vall[0:past, :] = cv_ref[0].astype(BF16)
    kall[past:, :] = jnp.zeros((LANE, LANE), BF16)
    vall[past:, :] = jnp.zeros((LANE, LANE), BF16)
    kall[past:past + ts, :] = kn_ref[0].astype(BF16)
    vall[past:past + ts, :] = vn_ref[0].astype(BF16)
    lane = lax.broadcasted_iota(jnp.int32, (1, LANE), 1)
    row = lax.broadcasted_iota(jnp.int32, (ts, past + LANE), 0)
    col = lax.broadcasted_iota(jnp.int32, (ts, past + LANE), 1)
    visible = col <= past + row
    q = q_ref[0]
    out = jnp.zeros((ts, LANE), F32)
    for hh in range(heads_per_block):
        head = pair * heads_per_block + hh
        in_head = (lane >= hh * head_dim) & (lane < (hh + 1) * head_dim)
        qm = jnp.where(in_head, q, jnp.zeros_like(q))
        f_cache = fct_ref[0, pl.ds(head, 1), :]
        total = f_cache[:, past - 1:past]
        f_keys = jnp.concatenate([f_cache, total + fnt_ref[0, pl.ds(head, 1), :]], axis=1)
        bias = ((total + _head_column(fn_ref[0], head)) - f_keys) * LOG2_E
        init = (jnp.full((ts, 1), NEG_BIG, F32), jnp.zeros((ts, 1), F32), jnp.zeros((ts, LANE), F32))
        _, l_i, acc = _softmax_step(qm, kall[...], vall[...], bias, init, mask=visible)
        out = jnp.where(in_head, acc / l_i, out)
    o_ref[0] = out.astype(o_ref.dtype)


def _attn_sample(qs, k_new, v_new, cache_k, cache_v, fn_col, fn_row, fc_row, head_dim):
    b, ts, w = qs.shape
    past = cache_k.shape[1]
    h = fn_col.shape[-1]
    assert ts <= LANE and past % LANE == 0
    new = pl.BlockSpec((1, ts, LANE), lambda i, j: (i, 0, j))
    old = pl.BlockSpec((1, past, LANE), lambda i, j: (i, 0, j))
    return pl.pallas_call(
        functools.partial(_attn_sample_kernel, head_dim=head_dim),
        grid=(b, w // LANE),
        in_specs=[new, new, new, old, old,
                  pl.BlockSpec((1, ts, h), lambda i, j: (i, 0, 0)),
                  pl.BlockSpec((1, h, LANE), lambda i, j: (i, 0, 0)),
                  pl.BlockSpec((1, h, past), lambda i, j: (i, 0, 0))],
        out_specs=new,
        out_shape=jax.ShapeDtypeStruct((b, ts, w), BF16),
        scratch_shapes=[pltpu.VMEM((past + LANE, LANE), BF16), pltpu.VMEM((past + LANE, LANE), BF16)],
        compiler_params=_params("parallel", "arbitrary"),
        name="attn_sample",
    )(qs, k_new, v_new, cache_k, cache_v, fn_col, fn_row, fc_row)


def _gate_weights(wa, wx):
    nb, bw, _ = wa.shape
    per = LANE // bw

    def chunked(w):
        w = w.reshape(nb // per, per, bw, bw)
        eye = jnp.eye(per, dtype=w.dtype)
        return jnp.einsum("cpij,pq->cpiqj", w, eye).reshape(nb // per, LANE, LANE)

    return jnp.concatenate([chunked(wa), chunked(wx)], axis=-1).astype(BF16)


def _pad_lanes(x, n):
    return jnp.pad(x, [(0, 0)] * (x.ndim - 1) + [(0, n - x.shape[-1])])


def _mixer(x, mod, w, conv0, h0, attend):
    b, t, d = x.shape
    xr, gr, qs, k, v, logf, ga, gb = _inproj(
        x, mod, w["mix_norm"], w["w_main"], w["w_fgt"], w["w_gate"], w["b_fgt"],
        w["n_heads"], w["q_scale"])
    a_out, conv_state, h_last = _rglru(xr, gr, conv0, h0, w["conv_w"], w["conv_b"], w["w_gates"],
                                       w["rg_ba"], w["rg_bx"], w["rg_lam"])
    att = attend(qs, k, v, logf)
    merge = (a_out, att, ga, gb, w["w_br_rnn"], w["w_br_att"], w["w_out"])
    return merge, k, v, logf, conv_state, h_last.reshape(b, -1)


def kernel(x_prompt, x_sample, cache_k, cache_v, cache_logf, state_conv, state_rglru, c_prompt, c_sample, w_ada, b_ada, ffn1_norm, ffn1_w1, ffn1_w3, ffn1_w2, mix_norm, w_in, b_fgt, conv_w, conv_b, rg_wa, rg_ba, rg_wx, rg_bx, rg_lam, w_br_rnn, w_br_att, w_out, ffn2_norm, ffn2_w1, ffn2_w3, ffn2_w2, final_norm, w_ada_f, b_ada_f):
    bp, seq, d = x_prompt.shape
    bs, dec_seq, _ = x_sample.shape
    depth = w_ada.shape[0]
    n_heads = b_fgt.shape[-1]
    head_dim = cache_k.shape[-1]
    d_rnn = conv_w.shape[-1]
    kw = conv_w.shape[1]
    att_w = n_heads * head_dim
    past = cache_k.shape[2]
    assert d_rnn == d and att_w == d and n_heads <= LANE
    assert w_in.shape[-1] == 2 * d_rnn + 3 * att_w + n_heads + 2 * d

    c_all = jnp.concatenate([c_prompt, c_sample], axis=0)
    mod_f = _ada_mod(c_all, w_ada_f, b_ada_f).reshape(bp + bs, 2, d)
    xp, xs = x_prompt, x_sample
    outs_p, outs_s = [], []
    for l in range(depth):
        mod = _ada_mod(c_all, w_ada[l], b_ada[l]).reshape(bp + bs, N_MOD, d)
        mp, ms = mod[:bp], mod[bp:]
        n_main = 2 * d_rnn + 3 * att_w
        w = {
            "mix_norm": mix_norm[l], "n_heads": n_heads, "q_scale": head_dim ** -0.5 * LOG2_E,
            "w_main": w_in[l, :, :n_main].astype(BF16),
            "w_fgt": _pad_lanes(w_in[l, :, n_main:n_main + n_heads], LANE).astype(BF16),
            "w_gate": w_in[l, :, n_main + n_heads:].astype(BF16),
            "b_fgt": _pad_lanes(b_fgt[l].reshape(1, n_heads), LANE),
            "conv_w": conv_w[l], "conv_b": conv_b[l], "w_gates": _gate_weights(rg_wa[l], rg_wx[l]),
            "rg_ba": rg_ba[l], "rg_bx": rg_bx[l], "rg_lam": rg_lam[l],
            "w_br_rnn": w_br_rnn[l].astype(BF16), "w_br_att": w_br_att[l].astype(BF16),
            "w_out": w_out[l].astype(BF16),
        }
        f1 = (ffn1_norm[l], ffn1_w1[l].astype(BF16), ffn1_w3[l].astype(BF16), ffn1_w2[l].astype(BF16))
        f2 = (ffn2_norm[l], ffn2_w1[l].astype(BF16), ffn2_w3[l].astype(BF16), ffn2_w2[l].astype(BF16))
        last = l == depth - 1

        xp = _ffn(xp, mp, *f1, base=0)
        xs = _ffn(xs, ms, *f1, base=0)

        def attend_prompt(qs, k, v, logf):
            return _attn_prompt(qs, k, v, _cumsum_lanes(jnp.swapaxes(logf, 1, 2)), head_dim)

        def attend_sample(qs, k, v, logf):
            fc_row = _cumsum_lanes(jnp.swapaxes(cache_logf[l], 1, 2))
            fn_row = _cumsum_lanes(_pad_lanes(jnp.swapaxes(logf, 1, 2), LANE))
            fn_col = jnp.swapaxes(fn_row[:, :, :dec_seq], 1, 2)
            return _attn_sample(qs, k, v, cache_k[l].reshape(bs, past, att_w),
                                cache_v[l].reshape(bs, past, att_w), fn_col, fn_row, fc_row, head_dim)

        merge_p, kp, vp, lp, cp, hp = _mixer(xp, mp, w, jnp.zeros((bp, kw - 1, d_rnn), F32),
                                             jnp.zeros((bp, d_rnn), F32), attend_prompt)
        merge_s, ks, vs, ls, cs, hs = _mixer(xs, ms, w, state_conv[l], state_rglru[l], attend_sample)

        xp = _ffn(xp, mp, *f2, base=6, merge=merge_p,
                  final=(final_norm, mod_f[:bp]) if last else None)
        xs = _ffn(xs, ms, *f2, base=6, merge=merge_s,
                  final=(final_norm, mod_f[bp:]) if last else None)
        heads_p = (bp, seq, n_heads, head_dim)
        heads_s = (bs, dec_seq, n_heads, head_dim)
        outs_p.append((kp.reshape(heads_p), vp.reshape(heads_p), lp, cp, hp))
        outs_s.append((ks.reshape(heads_s), vs.reshape(heads_s), ls, cs, hs))
    stack = lambda outs: tuple(jnp.stack(leaf) for leaf in zip(*outs))
    return (xp, xs) + stack(outs_p) + stack(outs_s)
```
